```python
import math
import jax, jax.numpy as jnp
from jax import lax
import numpy as np

D_MODEL = 2048
BATCH = 2
SEQ = 8192
DEPTH = 4

HEAD_DIM = 128
ATTN_Q_HEADS = 12
ATTN_KV_HEADS = 4
ATTN_GROUP = ATTN_Q_HEADS // ATTN_KV_HEADS
WINDOW = 128
ATTN_BLOCK = 128
RET_HEADS = 6
RET_DK = 128
RET_DV = 256
RET_CHUNK = 128
CROSS_HEADS = 4
MEM_LEN = 256
N_EXPERTS = 16
EXPERT_FF = 1024
CAPACITY_FACTOR = 2
NORM_EPS = 1e-6
NEG_INF = -1e30
N_ATTN_LAYERS = (DEPTH + 1) // 2
N_RET_LAYERS = DEPTH // 2

ATTN_Q_W = ATTN_Q_HEADS * HEAD_DIM
ATTN_KV_W = ATTN_KV_HEADS * HEAD_DIM
CROSS_W = CROSS_HEADS * HEAD_DIM
ATTN_IN_W = ATTN_Q_W + 2 * ATTN_KV_W + CROSS_W
RET_QK_W = RET_HEADS * RET_DK
RET_V_W = RET_HEADS * RET_DV
RET_IN_W = 2 * RET_QK_W + 2 * RET_V_W + CROSS_W

kernel_name = "hybrid_window_gqa_retention_ec_moe"


def rms_norm(x, gain):
    xf = x.astype(jnp.float32)
    y = xf * lax.rsqrt(jnp.mean(xf * xf, axis=-1, keepdims=True) + NORM_EPS)
    return (y * gain.astype(jnp.float32)).astype(x.dtype)


def alibi_slopes(n):
    def pow2_slopes(m):
        start = 2.0 ** (-(2.0 ** -(math.log2(m) - 3)))
        return [start ** (i + 1) for i in range(m)]
    if math.log2(n).is_integer():
        return pow2_slopes(n)
    closest = 2 ** math.floor(math.log2(n))
    return pow2_slopes(closest) + alibi_slopes(2 * closest)[0::2][: n - closest]


def windowed_gqa(q, k, v, sink):
    bsz, seq, _, d = q.shape
    nb = seq // ATTN_BLOCK
    c = ATTN_BLOCK
    qb = q.reshape(bsz, nb, c, ATTN_KV_HEADS, ATTN_GROUP, d)

    def band(t):
        tb = t.reshape(bsz, nb, c, ATTN_KV_HEADS, d)
        pad = jnp.pad(tb, ((0, 0), (1, 1), (0, 0), (0, 0), (0, 0)))
        return jnp.concatenate([pad[:, :-2], pad[:, 1:-1], pad[:, 2:]], axis=2)

    kb, vb = band(k), band(v)
    s = jnp.einsum('bnqhgd,bnshd->bhgnqs', qb, kb).astype(jnp.float32) * (d ** -0.5)
    qi = jnp.arange(c)[:, None]
    sj = jnp.arange(3 * c)[None, :]
    dist = jnp.abs(qi + c - sj)
    key_pos = jnp.arange(nb)[:, None] * c - c + jnp.arange(3 * c)[None, :]
    valid = (dist <= WINDOW)[None] & ((key_pos >= 0) & (key_pos < seq))[:, None, :]
    slopes = jnp.asarray(np.array(alibi_slopes(ATTN_Q_HEADS), np.float32)).reshape(ATTN_KV_HEADS, ATTN_GROUP)
    bias = -slopes[:, :, None, None, None] * dist.astype(jnp.float32)[None, None, None]
    logits = jnp.where(valid, s + bias, NEG_INF)
    sink_r = sink.astype(jnp.float32).reshape(ATTN_KV_HEADS, ATTN_GROUP)[None, :, :, None, None, None]
    m = jnp.maximum(jnp.max(logits, axis=-1, keepdims=True), sink_r)
    p = jnp.exp(logits - m)
    p = p / (jnp.sum(p, axis=-1, keepdims=True) + jnp.exp(sink_r - m))
    o = jnp.einsum('bhgnqs,bnshd->bnqhgd', p.astype(v.dtype), vb)
    return o.reshape(bsz, seq, ATTN_Q_HEADS * d)


def retention_chunkwise(q, k, v, log_gamma, include_diag):
    bsz, seq, heads, dk = q.shape
    dv = v.shape[-1]
    nc = seq // RET_CHUNK
    q = q.reshape(bsz, nc, RET_CHUNK, heads, dk)
    k = k.reshape(bsz, nc, RET_CHUNK, heads, dk)
    v = v.reshape(bsz, nc, RET_CHUNK, heads, dv)
    pos = jnp.arange(RET_CHUNK, dtype=jnp.float32)
    delta = pos[:, None] - pos[None, :]
    mask = (delta >= 0) if include_diag else (delta > 0)
    decay = jnp.where(mask[None], jnp.exp(log_gamma[:, None, None] * jnp.maximum(delta, 0.0)[None]), 0.0)
    scores = jnp.einsum('bnthd,bnshd->bnhts', q, k) * decay
    intra = jnp.einsum('bnhts,bnshe->bnthe', scores, v)
    zeta = jnp.exp(log_gamma[:, None] * (RET_CHUNK - 1.0 - pos)[None])
    chunk_kv = jnp.einsum('bnshd,hs,bnshe->bnhde', k, zeta, v)
    chunk_decay = jnp.exp(log_gamma * RET_CHUNK)[None, :, None, None]

    def step(state, kv_n):
        return chunk_decay * state + kv_n, state

    init = jnp.zeros((bsz, heads, dk, dv), jnp.float32)
    _, prev = lax.scan(step, init, jnp.moveaxis(chunk_kv, 1, 0))
    prev = jnp.moveaxis(prev, 0, 1)
    xi = jnp.exp(log_gamma[:, None] * (pos + 1.0)[None])
    inter = jnp.einsum('bnthd,ht,bnhde->bnthe', q, xi, prev)
    return (intra + inter).reshape(bsz, seq, heads, dv)


def bidirectional_retention(q, k, v, g, decay_fwd, decay_bwd, out_gain):
    bsz, seq, _ = q.shape
    qf = q.astype(jnp.float32).reshape(bsz, seq, RET_HEADS, RET_DK)
    kf = k.astype(jnp.float32).reshape(bsz, seq, RET_HEADS, RET_DK) * (RET_DK ** -0.5)
    vf = v.astype(jnp.float32).reshape(bsz, seq, RET_HEADS, RET_DV)
    lg_f = jax.nn.log_sigmoid(decay_fwd.astype(jnp.float32))
    lg_b = jax.nn.log_sigmoid(decay_bwd.astype(jnp.float32))
    fwd = retention_chunkwise(qf, kf, vf, lg_f, True)
    bwd = retention_chunkwise(qf[:, ::-1], kf[:, ::-1], vf[:, ::-1], lg_b, False)[:, ::-1]
    y = rms_norm(fwd + bwd, out_gain.reshape(RET_HEADS, RET_DV)).reshape(bsz, seq, RET_V_W)
    return (jax.nn.silu(g.astype(jnp.float32)) * y).astype(g.dtype)


def memory_cross_attention(qc, kc, vc):
    bsz, seq, _, d = qc.shape
    s = jnp.einsum('bshd,bmhd->bhsm', qc, kc).astype(jnp.float32) * (d ** -0.5)
    p = jax.nn.softmax(s, axis=-1).astype(vc.dtype)
    o = jnp.einsum('bhsm,bmhd->bshd', p, vc)
    return o.reshape(bsz, seq, CROSS_W)


def expert_choice_moe(h, w_router, w_gate, w_up, w_down):
    bsz, seq, d = h.shape
    cap = CAPACITY_FACTOR * seq // N_EXPERTS
    affinity = jax.nn.softmax((h @ w_router).astype(jnp.float32), axis=-1)
    gates, idx = lax.top_k(jnp.swapaxes(affinity, 1, 2), cap)
    xin = jax.vmap(lambda hb, ib: hb[ib])(h, idx)
    hid = jax.nn.silu(jnp.einsum('becd,edf->becf', xin, w_gate)) * jnp.einsum('becd,edf->becf', xin, w_up)
    y = jnp.einsum('becf,efd->becd', hid, w_down) * gates[..., None].astype(h.dtype)
    flat = (idx + (jnp.arange(bsz) * seq)[:, None, None]).reshape(-1)
    out = jnp.zeros((bsz * seq, d), h.dtype).at[flat].add(y.reshape(-1, d))
    return out.reshape(bsz, seq, d)


def setup_inputs(seed: int = 0) -> dict:
    key = jax.random.key(seed)
    ks = jax.random.split(key, 21)
    f32 = jnp.float32

    def nrm(k, shape, fan_in):
        return jax.random.normal(k, shape, f32) * (fan_in ** -0.5)

    def gain(k, shape):
        return 1.0 + 0.02 * jax.random.normal(k, shape, f32)

    decay_base = jnp.log(2.0 ** (5.0 + jnp.arange(RET_HEADS, dtype=f32)) - 1.0)
    return {
        "x": jax.random.normal(ks[0], (BATCH, SEQ, D_MODEL), f32),
        "mem": jax.random.normal(ks[1], (BATCH, MEM_LEN, D_MODEL), f32),
        "norm1_gain": gain(ks[2], (DEPTH, D_MODEL)),
        "w_in_attn": nrm(ks[3], (N_ATTN_LAYERS, D_MODEL, ATTN_IN_W), D_MODEL),
        "attn_q_gain": gain(ks[4], (N_ATTN_LAYERS, HEAD_DIM)),
        "attn_k_gain": gain(ks[5], (N_ATTN_LAYERS, HEAD_DIM)),
        "attn_sink": jax.random.normal(ks[6], (N_ATTN_LAYERS, ATTN_Q_HEADS), f32),
        "w_in_ret": nrm(ks[7], (N_RET_LAYERS, D_MODEL, RET_IN_W), D_MODEL),
        "ret_decay_fwd": decay_base[None] + 0.1 * jax.random.normal(ks[8], (N_RET_LAYERS, RET_HEADS), f32),
        "ret_decay_bwd": decay_base[None] + 0.1 * jax.random.normal(ks[9], (N_RET_LAYERS, RET_HEADS), f32),
        "ret_out_gain": gain(ks[10], (N_RET_LAYERS, RET_V_W)),
        "mem_norm_gain": gain(ks[11], (DEPTH, D_MODEL)),
        "w_mem_kv": nrm(ks[12], (DEPTH, D_MODEL, 2 * CROSS_W), D_MODEL),
        "cross_q_gain": gain(ks[13], (DEPTH, HEAD_DIM)),
        "cross_k_gain": gain(ks[14], (DEPTH, HEAD_DIM)),
        "w_out": nrm(ks[15], (DEPTH, D_MODEL, D_MODEL), D_MODEL),
        "norm2_gain": gain(ks[16], (DEPTH, D_MODEL)),
        "w_router": nrm(ks[17], (DEPTH, D_MODEL, N_EXPERTS), D_MODEL),
        "w_gate": nrm(ks[18], (DEPTH, N_EXPERTS, D_MODEL, EXPERT_FF), D_MODEL),
        "w_up": nrm(ks[19], (DEPTH, N_EXPERTS, D_MODEL, EXPERT_FF), D_MODEL),
        "w_down": nrm(ks[20], (DEPTH, N_EXPERTS, EXPERT_FF, D_MODEL), EXPERT_FF),
    }


def reference(x, mem, norm1_gain, w_in_attn, attn_q_gain, attn_k_gain, attn_sink,
              w_in_ret, ret_decay_fwd, ret_decay_bwd, ret_out_gain, mem_norm_gain,
              w_mem_kv, cross_q_gain, cross_k_gain, w_out, norm2_gain, w_router,
              w_gate, w_up, w_down):
    bsz, seq, _ = x.shape
    mem_len = mem.shape[1]
    for layer in range(DEPTH):
        h = rms_norm(x, norm1_gain[layer])
        kvm = rms_norm(mem, mem_norm_gain[layer]) @ w_mem_kv[layer]
        kc, vc = jnp.split(kvm, [CROSS_W], axis=-1)
        kc = rms_norm(kc.reshape(bsz, mem_len, CROSS_HEADS, HEAD_DIM), cross_k_gain[layer])
        vc = vc.reshape(bsz, mem_len, CROSS_HEADS, HEAD_DIM)
        li = layer // 2
        if layer % 2 == 0:
            proj = h @ w_in_attn[li]
            q, k, v, qc = jnp.split(proj, [ATTN_Q_W, ATTN_Q_W + ATTN_KV_W, ATTN_Q_W + 2 * ATTN_KV_W], axis=-1)
            q = rms_norm(q.reshape(bsz, seq, ATTN_Q_HEADS, HEAD_DIM), attn_q_gain[li])
            k = rms_norm(k.reshape(bsz, seq, ATTN_KV_HEADS, HEAD_DIM), attn_k_gain[li])
            v = v.reshape(bsz, seq, ATTN_KV_HEADS, HEAD_DIM)
            mix = windowed_gqa(q, k, v, attn_sink[li])
        else:
            proj = h @ w_in_ret[li]
            q, k, v, g, qc = jnp.split(
                proj, [RET_QK_W, 2 * RET_QK_W, 2 * RET_QK_W + RET_V_W, 2 * RET_QK_W + 2 * RET_V_W], axis=-1)
            mix = bidirectional_retention(q, k, v, g, ret_decay_fwd[li], ret_decay_bwd[li], ret_out_gain[li])
        qc = rms_norm(qc.reshape(bsz, seq, CROSS_HEADS, HEAD_DIM), cross_q_gain[layer])
        cross = memory_cross_attention(qc, kc, vc)
        x = x + jnp.concatenate([mix, cross], axis=-1) @ w_out[layer]
        x = x + expert_choice_moe(rms_norm(x, norm2_gain[layer]), w_router[layer],
                                  w_gate[layer], w_up[layer], w_down[layer])
    return x
```

```python
import functools
import math

import jax
import jax.numpy as jnp
import numpy as np
from jax import lax
from jax.experimental import pallas as pl
from jax.experimental.pallas import tpu as pltpu

F32 = jnp.float32
BF16 = jnp.bfloat16

HEAD_DIM = 128
ATTN_Q_HEADS = 12
ATTN_KV_HEADS = 4
ATTN_GROUP = ATTN_Q_HEADS // ATTN_KV_HEADS
WINDOW = 128
RET_HEADS = 6
RET_DK = 128
RET_DV = 256
CROSS_HEADS = 4
N_EXPERTS = 16
CAPACITY_FACTOR = 2
NORM_EPS = 1e-6
NEG_INF = -1e30

ATTN_Q_W = ATTN_Q_HEADS * HEAD_DIM
ATTN_KV_W = ATTN_KV_HEADS * HEAD_DIM
CROSS_W = CROSS_HEADS * HEAD_DIM
RET_QK_W = RET_HEADS * RET_DK
RET_V_W = RET_HEADS * RET_DV

LANES = 128
VMEM_LIMIT = 56 * 1024 * 1024

RET_CHUNK = 256
MOE_ROWS = 512
COMBINE_TOKENS = 512
COMBINE_CHUNK = 256
COMBINE_ALIGN = 16


def _cparams(semantics):
    return pltpu.CompilerParams(dimension_semantics=semantics, vmem_limit_bytes=VMEM_LIMIT)


def _alibi_slopes(n):
    def pow2_slopes(m):
        start = 2.0 ** (-(2.0 ** -(math.log2(m) - 3)))
        return [start ** (i + 1) for i in range(m)]
    if math.log2(n).is_integer():
        return pow2_slopes(n)
    closest = 2 ** math.floor(math.log2(n))
    return pow2_slopes(closest) + _alibi_slopes(2 * closest)[0::2][: n - closest]


def _norm_matmul_kernel(x_ref, g_ref, w_ref, ps_ref, o_ref, h_ref, *, norm_tiles):
    j = pl.program_id(1)

    @pl.when(j == 0)
    def _():
        x = x_ref[...]
        ms = jnp.mean(x * x, axis=-1, keepdims=True)
        h_ref[...] = (x * lax.rsqrt(ms + NORM_EPS) * g_ref[...]).astype(BF16)

    acc = jnp.dot(h_ref[...], w_ref[...], preferred_element_type=F32)
    tn = acc.shape[1]
    lo1, hi1, lo2, hi2 = norm_tiles
    normed = ((j >= lo1) & (j < hi1)) | ((j >= lo2) & (j < hi2))

    @pl.when(normed)
    def _():
        for c in range(tn // HEAD_DIM):
            sl = slice(c * HEAD_DIM, (c + 1) * HEAD_DIM)
            a = acc[:, sl]
            ms = jnp.mean(a * a, axis=-1, keepdims=True)
            o_ref[:, sl] = (a * lax.rsqrt(ms + NORM_EPS) * ps_ref[:, sl]).astype(o_ref.dtype)

    @pl.when(jnp.logical_not(normed))
    def _():
        o_ref[...] = (acc * ps_ref[...]).astype(o_ref.dtype)


def _norm_matmul(x, gain, w, post_scale, norm_cols, *, tm, tn=512):
    m, d = x.shape
    n = w.shape[1]
    assert m % tm == 0 and n % tn == 0 and all(c % tn == 0 for c in norm_cols)
    norm_tiles = tuple(c // tn for c in norm_cols)
    return pl.pallas_call(
        functools.partial(_norm_matmul_kernel, norm_tiles=norm_tiles),
        out_shape=jax.ShapeDtypeStruct((m, n), BF16),
        grid=(m // tm, n // tn),
        in_specs=[
            pl.BlockSpec((tm, d), lambda i, j: (i, 0)),
            pl.BlockSpec((1, d), lambda i, j: (0, 0)),
            pl.BlockSpec((d, tn), lambda i, j: (0, j)),
            pl.BlockSpec((1, tn), lambda i, j: (0, j)),
        ],
        out_specs=pl.BlockSpec((tm, tn), lambda i, j: (i, j)),
        scratch_shapes=[pltpu.VMEM((tm, d), BF16)],
        compiler_params=_cparams(("parallel", "arbitrary")),
        name="norm_matmul",
    )(x, gain.reshape(1, d), w, post_scale.reshape(1, n))


def _win_attn_kernel(sink_ref, q_ref, kp_ref, kc_ref, kn_ref, vp_ref, vc_ref, vn_ref, o_ref, *, slopes):
    n = pl.program_id(1)
    nb = pl.num_programs(1)
    c = WINDOW
    qi = lax.broadcasted_iota(jnp.int32, (c, 3 * c), 0)
    sj = lax.broadcasted_iota(jnp.int32, (c, 3 * c), 1)
    dist = jnp.abs(qi + c - sj)
    key_lo = jnp.where(n > 0, 0, c)
    key_hi = jnp.where(n < nb - 1, 3 * c, 2 * c)
    valid = (dist <= WINDOW) & (sj >= key_lo) & (sj < key_hi)
    distf = dist.astype(F32)
    for h in range(ATTN_KV_HEADS):
        hs = slice(h * HEAD_DIM, (h + 1) * HEAD_DIM)
        kb = jnp.concatenate([kp_ref[:, hs], kc_ref[:, hs], kn_ref[:, hs]], axis=0)
        vb = jnp.concatenate([vp_ref[:, hs], vc_ref[:, hs], vn_ref[:, hs]], axis=0)
        for g in range(ATTN_GROUP):
            qh = h * ATTN_GROUP + g
            qs = slice(qh * HEAD_DIM, (qh + 1) * HEAD_DIM)
            s = lax.dot_general(q_ref[:, qs], kb, (((1,), (1,)), ((), ())),
                                preferred_element_type=F32)
            logits = jnp.where(valid, s - slopes[qh] * distf, NEG_INF)
            sink = sink_ref[qh]
            m = jnp.maximum(jnp.max(logits, axis=-1, keepdims=True), sink)
            p = jnp.exp(logits - m)
            denom = jnp.sum(p, axis=-1, keepdims=True) + jnp.exp(sink - m)
            o = jnp.dot(p.astype(BF16), vb, preferred_element_type=F32)
            o_ref[:, qs] = (o / denom).astype(o_ref.dtype)


def _win_attn(proj, sink, bsz, seq):
    c = WINDOW
    nb = seq // c
    kcol = ATTN_Q_W // ATTN_KV_W
    vcol = kcol + 1
    slopes = tuple(float(s) for s in np.array(_alibi_slopes(ATTN_Q_HEADS), np.float32))

    def kv_spec(col, shift):
        def imap(b, n):
            return (b * nb + jnp.clip(n + shift, 0, nb - 1), col)
        return pl.BlockSpec((c, ATTN_KV_W), imap)

    return pl.pallas_call(
        functools.partial(_win_attn_kernel, slopes=slopes),
        out_shape=jax.ShapeDtypeStruct((bsz * seq, ATTN_Q_W), BF16),
        grid=(bsz, nb),
        in_specs=[
            pl.BlockSpec(memory_space=pltpu.SMEM),
            pl.BlockSpec((c, ATTN_Q_W), lambda b, n: (b * nb + n, 0)),
            kv_spec(kcol, -1), kv_spec(kcol, 0), kv_spec(kcol, 1),
            kv_spec(vcol, -1), kv_spec(vcol, 0), kv_spec(vcol, 1),
        ],
        out_specs=pl.BlockSpec((c, ATTN_Q_W), lambda b, n: (b * nb + n, 0)),
        compiler_params=_cparams(("parallel", "arbitrary")),
        name="win_attn",
    )(sink, proj, proj, proj, proj, proj, proj, proj)


def _cross_attn_kernel(q_ref, kv_ref, o_ref):
    for h in range(CROSS_HEADS):
        hs = slice(h * HEAD_DIM, (h + 1) * HEAD_DIM)
        vs = slice(CROSS_W + h * HEAD_DIM, CROSS_W + (h + 1) * HEAD_DIM)
        s = lax.dot_general(q_ref[:, hs], kv_ref[:, hs], (((1,), (1,)), ((), ())),
                            preferred_element_type=F32)
        m = jnp.max(s, axis=-1, keepdims=True)
        p = jnp.exp(s - m)
        denom = jnp.sum(p, axis=-1, keepdims=True)
        o = jnp.dot(p.astype(BF16), kv_ref[:, vs], preferred_element_type=F32)
        o_ref[:, hs] = (o / denom).astype(o_ref.dtype)


def _cross_attn(proj, qcol, kvm, bsz, seq, mem_len, *, tq=512):
    nq = seq // tq
    return pl.pallas_call(
        _cross_attn_kernel,
        out_shape=jax.ShapeDtypeStruct((bsz * seq, CROSS_W), BF16),
        grid=(bsz, nq),
        in_specs=[
            pl.BlockSpec((tq, CROSS_W), lambda b, i: (b * nq + i, qcol)),
            pl.BlockSpec((mem_len, 2 * CROSS_W), lambda b, i: (b, 0)),
        ],
        out_specs=pl.BlockSpec((tq, CROSS_W), lambda b, i: (b * nq + i, 0)),
        compiler_params=_cparams(("parallel", "arbitrary")),
        name="cross_attn",
    )(proj, kvm)


def _log_sigmoid(x):
    return jnp.minimum(x, 0.0) - jnp.log1p(jnp.exp(-jnp.abs(x)))


def _retention_kernel(df_ref, db_ref, qf_ref, kf_ref, vf_ref, qb_ref, kb_ref, vb_ref,
                      of_ref, ob_ref, sf_ref, sb_ref):
    h = pl.program_id(1)
    n = pl.program_id(2)
    c = qf_ref.shape[0]

    @pl.when(n == 0)
    def _():
        sf_ref[...] = jnp.zeros_like(sf_ref)
        sb_ref[...] = jnp.zeros_like(sb_ref)

    ti = lax.broadcasted_iota(jnp.int32, (c, c), 0)
    si = lax.broadcasted_iota(jnp.int32, (c, c), 1)
    row = lax.broadcasted_iota(jnp.int32, (c, 1), 0).astype(F32)

    def one_direction(lg, q_ref, k_ref, v_ref, o_ref, s_ref, forward):
        delta = (ti - si) if forward else (si - ti)
        mask = (delta >= 0) if forward else (delta > 0)
        decay = jnp.where(mask, jnp.exp(lg * jnp.maximum(delta, 0).astype(F32)), 0.0)
        q = q_ref[...]
        k = k_ref[...]
        v = v_ref[...]
        scores = lax.dot_general(q, k, (((1,), (1,)), ((), ())), preferred_element_type=F32) * decay
        intra = jnp.dot(scores.astype(BF16), v, preferred_element_type=F32)
        xi = jnp.exp(lg * ((row + 1.0) if forward else (c - row)))
        zeta = jnp.exp(lg * ((c - 1.0 - row) if forward else row))
        state = s_ref[...]
        inter = jnp.dot((q.astype(F32) * xi).astype(BF16), state.astype(BF16),
                        preferred_element_type=F32)
        o_ref[...] = intra + inter
        kz = (k.astype(F32) * zeta).T.astype(BF16)
        s_ref[...] = jnp.exp(lg * c) * state + jnp.dot(kz, v, preferred_element_type=F32)

    lg_f = _log_sigmoid(jnp.full((1, 1), df_ref[h], F32))
    lg_b = _log_sigmoid(jnp.full((1, 1), db_ref[h], F32))
    one_direction(lg_f, qf_ref, kf_ref, vf_ref, of_ref, sf_ref, True)
    one_direction(lg_b, qb_ref, kb_ref, vb_ref, ob_ref, sb_ref, False)


def _retention(proj, decay_fwd, decay_bwd, bsz, seq):
    c = RET_CHUNK
    nc = seq // c
    kcol0 = RET_QK_W // RET_DK
    vcol0 = 2 * RET_QK_W // RET_DV

    def specs(rev):
        def chunk(b, n):
            return b * nc + ((nc - 1 - n) if rev else n)
        return [
            pl.BlockSpec((c, RET_DK), lambda b, h, n: (chunk(b, n), h)),
            pl.BlockSpec((c, RET_DK), lambda b, h, n: (chunk(b, n), kcol0 + h)),
            pl.BlockSpec((c, RET_DV), lambda b, h, n: (chunk(b, n), vcol0 + h)),
        ], pl.BlockSpec((c, RET_DV), lambda b, h, n: (chunk(b, n), h))

    in_f, out_f = specs(False)
    in_b, out_b = specs(True)
    smem = pl.BlockSpec(memory_space=pltpu.SMEM)
    out = jax.ShapeDtypeStruct((bsz * seq, RET_V_W), F32)
    return pl.pallas_call(
        _retention_kernel,
        out_shape=(out, out),
        grid=(bsz, RET_HEADS, nc),
        in_specs=[smem, smem] + in_f + in_b,
        out_specs=(out_f, out_b),
        scratch_shapes=[pltpu.VMEM((RET_DK, RET_DV), F32), pltpu.VMEM((RET_DK, RET_DV), F32)],
        compiler_params=_cparams(("parallel", "parallel", "arbitrary")),
        name="retention",
    )(decay_fwd, decay_bwd, proj, proj, proj, proj, proj, proj)


def _split_bf16(a):
    hi = a.astype(BF16)
    lo = (a - hi.astype(F32)).astype(BF16)
    return hi, lo


def _finish_out_proj(x_ref, mix, cross_ref, w_ref, g2_ref, wr_ref, xo_ref, h2_ref, aff_ref):
    mix_w = mix.shape[1]
    y = jnp.dot(mix, w_ref[:mix_w, :], preferred_element_type=F32)
    y = y + jnp.dot(cross_ref[...], w_ref[mix_w:, :], preferred_element_type=F32)
    x = x_ref[...] + y
    xo_ref[...] = x
    ms = jnp.mean(x * x, axis=-1, keepdims=True)
    h2 = x * lax.rsqrt(ms + NORM_EPS) * g2_ref[...]
    h2_ref[...] = h2
    h_hi, h_lo = _split_bf16(h2)
    w_hi, w_lo = _split_bf16(wr_ref[...])
    logits = (jnp.dot(h_hi, w_hi, preferred_element_type=F32)
              + jnp.dot(h_lo, w_hi, preferred_element_type=F32)
              + jnp.dot(h_hi, w_lo, preferred_element_type=F32))
    m = jnp.max(logits, axis=-1, keepdims=True)
    p = jnp.exp(logits - m)
    aff_ref[...] = p / jnp.sum(p, axis=-1, keepdims=True)


def _out_proj_attn_kernel(x_ref, mix_ref, cross_ref, w_ref, g2_ref, wr_ref, xo_ref, h2_ref, aff_ref):
    _finish_out_proj(x_ref, mix_ref[...], cross_ref, w_ref, g2_ref, wr_ref, xo_ref, h2_ref, aff_ref)


def _out_proj_ret_kernel(x_ref, fwd_ref, bwd_ref, gate_ref, og_ref, cross_ref, w_ref, g2_ref, wr_ref,
                         xo_ref, h2_ref, aff_ref):
    parts = []
    for h in range(RET_HEADS):
        hs = slice(h * RET_DV, (h + 1) * RET_DV)
        y = fwd_ref[:, hs] + bwd_ref[:, hs]
        ms = jnp.mean(y * y, axis=-1, keepdims=True)
        y = y * lax.rsqrt(ms + NORM_EPS) * og_ref[:, hs]
        g = gate_ref[:, hs].astype(F32)
        parts.append((g * jax.nn.sigmoid(g) * y).astype(BF16))
    mix = jnp.concatenate(parts, axis=-1)
    _finish_out_proj(x_ref, mix, cross_ref, w_ref, g2_ref, wr_ref, xo_ref, h2_ref, aff_ref)


def _out_proj(x, mix_inputs, cross, w_out, g2, w_router, *, tm=256):
    m, d = x.shape
    row = lambda i: (i, 0)
    const = lambda i: (0, 0)
    tail_specs = [
        pl.BlockSpec((tm, CROSS_W), row),
        pl.BlockSpec((d, d), const),
        pl.BlockSpec((1, d), const),
        pl.BlockSpec((d, N_EXPERTS), const),
    ]
    tail_args = (cross, w_out, g2.reshape(1, d), w_router)
    if len(mix_inputs) == 1:
        kernel = _out_proj_attn_kernel
        specs = [pl.BlockSpec((tm, d), row), pl.BlockSpec((tm, ATTN_Q_W), row)] + tail_specs
        args = (x, mix_inputs[0]) + tail_args
    else:
        fwd, bwd, proj, gate_col, out_gain = mix_inputs
        kernel = _out_proj_ret_kernel
        specs = [
            pl.BlockSpec((tm, d), row),
            pl.BlockSpec((tm, RET_V_W), row),
            pl.BlockSpec((tm, RET_V_W), row),
            pl.BlockSpec((tm, RET_V_W), lambda i: (i, gate_col)),
            pl.BlockSpec((1, RET_V_W), const),
        ] + tail_specs
        args = (x, fwd, bwd, proj, out_gain.reshape(1, RET_V_W)) + tail_args
    return pl.pallas_call(
        kernel,
        out_shape=(jax.ShapeDtypeStruct((m, d), F32), jax.ShapeDtypeStruct((m, d), F32),
                   jax.ShapeDtypeStruct((m, N_EXPERTS), F32)),
        grid=(m // tm,),
        in_specs=specs,
        out_specs=(pl.BlockSpec((tm, d), row), pl.BlockSpec((tm, d), row),
                   pl.BlockSpec((tm, N_EXPERTS), row)),
        compiler_params=_cparams(("parallel",)),
        name="out_proj",
    )(*args)


def _lane_cumsum(m, tri):
    r, s = m.shape
    carry = jnp.zeros((r, 1), F32)
    out = []
    for c in range(s // LANES):
        blk = m[:, c * LANES:(c + 1) * LANES]
        inc = jnp.dot(blk.astype(BF16), tri, preferred_element_type=F32) + carry
        out.append(inc)
        carry = inc[:, LANES - 1:LANES]
    return jnp.concatenate(out, axis=-1)


def _route_kernel(aff_ref, idx_ref, pos_ref, gate_ref, cum_ref, *, cap):
    aff = aff_ref[0]
    e, s = aff.shape
    bits = pltpu.bitcast(aff, jnp.int32)
    capf = float(cap)

    def count(mask):
        return jnp.sum(jnp.where(mask, 1.0, 0.0), axis=-1, keepdims=True)

    def search(_, lohi):
        lo, hi = lohi
        mid = lo + ((hi - lo + 1) >> 1)
        ok = count(bits >= mid) >= capf
        return jnp.where(ok, mid, lo), jnp.where(ok, hi, mid - 1)

    lo0 = jnp.zeros((e, 1), jnp.int32)
    hi0 = jnp.full((e, 1), 0x7F800000, jnp.int32)
    thr, _ = lax.fori_loop(0, 31, search, (lo0, hi0))

    ri = lax.broadcasted_iota(jnp.int32, (LANES, LANES), 0)
    ci = lax.broadcasted_iota(jnp.int32, (LANES, LANES), 1)
    tri = jnp.where(ri <= ci, 1.0, 0.0).astype(BF16)

    gt = bits > thr
    eq = bits == thr
    need = capf - count(gt)
    eqf = jnp.where(eq, 1.0, 0.0)
    tie_rank = _lane_cumsum(eqf, tri) - eqf
    sel = gt | (eq & (tie_rank < need))
    self_ = jnp.where(sel, 1.0, 0.0)
    cum = _lane_cumsum(self_, tri)
    pos = (cum - self_).astype(jnp.int32)
    pos_ref[0] = jnp.where(sel, pos, -1 - pos)
    gate_ref[0] = jnp.where(sel, aff, 0.0)
    cum_ref[...] = cum

    ones = jnp.ones((8, s), BF16)

    def expert_list(ex, carry):
        row = cum_ref[pl.ds(ex, 1), :]
        for kt in range(cap // LANES):
            kcol = (lax.broadcasted_iota(jnp.int32, (LANES, 1), 0) + kt * LANES).astype(F32)
            below = jnp.where(row <= kcol, 1.0, 0.0).astype(BF16)
            cnt = lax.dot_general(ones, below, (((1,), (1,)), ((), ())),
                                  preferred_element_type=F32)
            idx_ref[0, pl.ds(pl.multiple_of(ex * 8, 8), 8), kt * LANES:(kt + 1) * LANES] = cnt.astype(jnp.int32)
        return carry

    lax.fori_loop(0, e, expert_list, 0)


def _route(aff_t, cap):
    bsz, e, s = aff_t.shape
    blk = pl.BlockSpec((1, e, s), lambda b: (b, 0, 0))
    idx8, pos, gate = pl.pallas_call(
        functools.partial(_route_kernel, cap=cap),
        out_shape=(jax.ShapeDtypeStruct((bsz, e * 8, cap), jnp.int32),
                   jax.ShapeDtypeStruct((bsz, e, s), jnp.int32),
                   jax.ShapeDtypeStruct((bsz, e, s), F32)),
        grid=(bsz,),
        in_specs=[blk],
        out_specs=(pl.BlockSpec((1, e * 8, cap), lambda b: (b, 0, 0)), blk, blk),
        scratch_shapes=[pltpu.VMEM((e, s), F32)],
        compiler_params=_cparams(("parallel",)),
        name="route",
    )(aff_t)
    return idx8[:, ::8, :], pos, gate


def _expert_kernel(idx_ref, h_hbm, wg_ref, wu_ref, wd_ref, y_ref, xbuf, sem, *, seq, cap):
    e = pl.program_id(0)
    b = pl.program_id(1)
    r = pl.program_id(2)
    rows = xbuf.shape[0]
    base = (b * pl.num_programs(0) + e) * cap + r * rows

    def row_copy(i):
        tok = b * seq + idx_ref[base + i]
        return pltpu.make_async_copy(h_hbm.at[pl.ds(tok, 1), :], xbuf.at[pl.ds(i, 1), :], sem)

    def issue(i, carry):
        row_copy(i).start()
        return carry

    lax.fori_loop(0, rows, issue, 0)
    pltpu.make_async_copy(h_hbm.at[pl.ds(0, rows), :], xbuf, sem).wait()

    x = xbuf[...].astype(BF16)
    hg = jnp.dot(x, wg_ref[0], preferred_element_type=F32)
    hu = jnp.dot(x, wu_ref[0], preferred_element_type=F32)
    hid = (hg * jax.nn.sigmoid(hg) * hu).astype(BF16)
    y_ref[0, 0] = jnp.dot(hid, wd_ref[0], preferred_element_type=F32).astype(y_ref.dtype)


def _experts(idx, h2, wg, wu, wd, seq):
    bsz, e, cap = idx.shape
    d = h2.shape[1]
    ff = wg.shape[2]
    rows = MOE_ROWS
    grid_spec = pltpu.PrefetchScalarGridSpec(
        num_scalar_prefetch=1,
        grid=(e, bsz, cap // rows),
        in_specs=[
            pl.BlockSpec(memory_space=pl.ANY),
            pl.BlockSpec((1, d, ff), lambda ei, b, r, idx: (ei, 0, 0)),
            pl.BlockSpec((1, d, ff), lambda ei, b, r, idx: (ei, 0, 0)),
            pl.BlockSpec((1, ff, d), lambda ei, b, r, idx: (ei, 0, 0)),
        ],
        out_specs=pl.BlockSpec((1, 1, rows, d), lambda ei, b, r, idx: (b, ei, r, 0)),
        scratch_shapes=[pltpu.VMEM((rows, d), F32), pltpu.SemaphoreType.DMA],
    )
    return pl.pallas_call(
        functools.partial(_expert_kernel, seq=seq, cap=cap),
        out_shape=jax.ShapeDtypeStruct((bsz, e, cap, d), BF16),
        grid_spec=grid_spec,
        compiler_params=_cparams(("arbitrary", "arbitrary", "arbitrary")),
        name="experts",
    )(idx.reshape(-1), h2, wg, wu, wd)


def _combine_kernel(off_ref, x_ref, pos_ref, gate_ref, y_hbm, o_ref, ybuf, sem, *, cap, n_tiles):
    b = pl.program_id(0)
    i = pl.program_id(1)
    tt = x_ref.shape[0]
    ch = ybuf.shape[1]
    n_exp = pos_ref.shape[1]
    last_start = cap - ch
    o_ref[...] = x_ref[...]
    col = lax.broadcasted_iota(jnp.int32, (tt, ch), 1)

    def chunk_copy(ex, start, slot):
        return pltpu.make_async_copy(y_hbm.at[b, ex, pl.ds(start, ch), :], ybuf.at[slot], sem.at[slot])

    def first_row(ex):
        o = off_ref[(b * (n_tiles + 1) + i) * n_exp + ex]
        return (o // COMBINE_ALIGN) * COMBINE_ALIGN

    def chunk_start(lo):
        return pl.multiple_of(jnp.minimum(lo, last_start), COMBINE_ALIGN)

    chunk_copy(0, chunk_start(first_row(0)), 0).start()
    for ex in range(n_exp):
        slot = ex % 2
        end = off_ref[(b * (n_tiles + 1) + i + 1) * n_exp + ex]
        lo0 = first_row(ex)
        if ex + 1 < n_exp:
            chunk_copy(ex + 1, chunk_start(first_row(ex + 1)), 1 - slot).start()
        p = pos_ref[:, ex:ex + 1]
        g = gate_ref[:, ex:ex + 1]

        def accumulate(lo, start, slot=slot, p=p, g=g):
            hit = (p >= lo) & (p < lo + ch) & ((p - start) == col)
            onehot = jnp.where(hit, 1.0, 0.0).astype(BF16)
            z = jnp.dot(onehot, ybuf[slot], preferred_element_type=F32)
            o_ref[...] += g * z

        chunk_copy(ex, chunk_start(lo0), slot).wait()
        accumulate(lo0, chunk_start(lo0))

        n_extra = jnp.maximum(end - lo0 - 1, 0) // ch

        def extra(c, carry, ex=ex, slot=slot, lo0=lo0, accumulate=accumulate):
            lo = lo0 + (c + 1) * ch
            cp = chunk_copy(ex, chunk_start(lo), slot)
            cp.start()
            cp.wait()
            accumulate(lo, chunk_start(lo))
            return carry

        lax.fori_loop(0, n_extra, extra, 0)


def _combine(x, pos, gate, offsets, y, bsz, seq):
    d = x.shape[1]
    e, cap = y.shape[1], y.shape[2]
    tt = COMBINE_TOKENS
    n_tiles = seq // tt
    row = lambda b, i, off: (b * n_tiles + i, 0)
    grid_spec = pltpu.PrefetchScalarGridSpec(
        num_scalar_prefetch=1,
        grid=(bsz, n_tiles),
        in_specs=[
            pl.BlockSpec((tt, d), row),
            pl.BlockSpec((tt, e), row),
            pl.BlockSpec((tt, e), row),
            pl.BlockSpec(memory_space=pl.ANY),
        ],
        out_specs=pl.BlockSpec((tt, d), row),
        scratch_shapes=[pltpu.VMEM((2, COMBINE_CHUNK, d), BF16), pltpu.SemaphoreType.DMA((2,))],
    )
    return pl.pallas_call(
        functools.partial(_combine_kernel, cap=cap, n_tiles=n_tiles),
        out_shape=jax.ShapeDtypeStruct(x.shape, F32),
        grid_spec=grid_spec,
        compiler_params=_cparams(("arbitrary", "arbitrary")),
        name="combine",
    )(offsets.reshape(-1), x, pos, gate, y)


def _moe(x, h2, aff, wg, wu, wd, bsz, seq):
    cap = CAPACITY_FACTOR * seq // N_EXPERTS
    aff_t = jnp.swapaxes(aff.reshape(bsz, seq, N_EXPERTS), 1, 2)
    idx, pos_t, gate_t = _route(aff_t, cap)
    y = _experts(idx, h2, wg, wu, wd, seq)
    first = pos_t[:, :, ::COMBINE_TOKENS]
    offsets = jnp.concatenate(
        [jnp.where(first >= 0, first, -1 - first), jnp.full((bsz, N_EXPERTS, 1), cap, jnp.int32)], axis=-1)
    offsets = jnp.swapaxes(offsets, 1, 2)
    pos = jnp.swapaxes(pos_t, 1, 2).reshape(bsz * seq, N_EXPERTS)
    gate = jnp.swapaxes(gate_t, 1, 2).reshape(bsz * seq, N_EXPERTS)
    return _combine(x, pos, gate, offsets, y, bsz, seq)


def kernel(x, mem, norm1_gain, w_in_attn, attn_q_gain, attn_k_gain, attn_sink, w_in_ret, ret_decay_fwd,
           ret_decay_bwd, ret_out_gain, mem_norm_gain, w_mem_kv, cross_q_gain, cross_k_gain, w_out,
           norm2_gain, w_router, w_gate, w_up, w_down):
    bsz, seq, d = x.shape
    mem_len = mem.shape[1]
    depth = norm1_gain.shape[0]
    xf = x.reshape(bsz * seq, d)
    memf = mem.reshape(bsz * mem_len, d)
    scale = HEAD_DIM ** -0.5
    ones = lambda n: jnp.ones((n,), F32)
    tile = lambda g, n: jnp.tile(g, n)
    for layer in range(depth):
        li = layer // 2
        kv_scale = jnp.concatenate([tile(cross_k_gain[layer], CROSS_HEADS), ones(CROSS_W)])
        kvm = _norm_matmul(memf, mem_norm_gain[layer], w_mem_kv[layer].astype(BF16), kv_scale,
                           (0, CROSS_W, 0, 0), tm=bsz * mem_len)
        qc_scale = tile(cross_q_gain[layer] * scale, CROSS_HEADS)
        if layer % 2 == 0:
            post = jnp.concatenate([tile(attn_q_gain[li] * scale, ATTN_Q_HEADS),
                                    tile(attn_k_gain[li], ATTN_KV_HEADS), ones(ATTN_KV_W), qc_scale])
            q_end = ATTN_Q_W + ATTN_KV_W
            qc_start = q_end + ATTN_KV_W
            proj = _norm_matmul(xf, norm1_gain[layer], w_in_attn[li].astype(BF16), post,
                                (0, q_end, qc_start, qc_start + CROSS_W), tm=512)
            mix_inputs = (_win_attn(proj, attn_sink[li], bsz, seq),)
        else:
            post = jnp.concatenate([ones(RET_QK_W), jnp.full((RET_QK_W,), RET_DK ** -0.5, F32),
                                    ones(2 * RET_V_W), qc_scale])
            qc_start = 2 * RET_QK_W + 2 * RET_V_W
            proj = _norm_matmul(xf, norm1_gain[layer], w_in_ret[li].astype(BF16), post,
                                (qc_start, qc_start + CROSS_W, 0, 0), tm=512)
            fwd, bwd = _retention(proj, ret_decay_fwd[li], ret_decay_bwd[li], bsz, seq)
            gate_col = (2 * RET_QK_W + RET_V_W) // RET_V_W
            mix_inputs = (fwd, bwd, proj, gate_col, ret_out_gain[li])
        cross = _cross_attn(proj, qc_start // CROSS_W, kvm, bsz, seq, mem_len)
        xf, h2, aff = _out_proj(xf, mix_inputs, cross, w_out[layer].astype(BF16), norm2_gain[layer],
                                w_router[layer])
        xf = _moe(xf, h2, aff, w_gate[layer].astype(BF16), w_up[layer].astype(BF16),
                  w_down[layer].astype(BF16), bsz, seq)
    return xf.reshape(bsz, seq, d)
```

```python
import functools
import math

import jax
import jax.numpy as jnp
import numpy as np
from jax import lax
from jax.experimental import pallas as pl
from jax.experimental.pallas import tpu as pltpu

F32 = jnp.float32
BF16 = jnp.bfloat16

HEAD_DIM = 128
ATTN_Q_HEADS = 12
ATTN_KV_HEADS = 4
ATTN_GROUP = ATTN_Q_HEADS // ATTN_KV_HEADS
WINDOW = 128
RET_HEADS = 6
RET_DK = 128
RET_DV = 256
CROSS_HEADS = 4
N_EXPERTS = 16
CAPACITY_FACTOR = 2
NORM_EPS = 1e-6
NEG_INF = -1e30

ATTN_Q_W = ATTN_Q_HEADS * HEAD_DIM
ATTN_KV_W = ATTN_KV_HEADS * HEAD_DIM
CROSS_W = CROSS_HEADS * HEAD_DIM
RET_QK_W = RET_HEADS * RET_DK
RET_V_W = RET_HEADS * RET_DV

LANES = 128
VMEM_LIMIT = 56 * 1024 * 1024

RET_CHUNK = 256
MOE_ROWS = 512
COMBINE_TOKENS = 256
COMBINE_CHUNK = 64
COMBINE_ALIGN = 16


def _cparams(semantics):
    return pltpu.CompilerParams(dimension_semantics=semantics, vmem_limit_bytes=VMEM_LIMIT)


def _alibi_slopes(n):
    def pow2_slopes(m):
        start = 2.0 ** (-(2.0 ** -(math.log2(m) - 3)))
        return [start ** (i + 1) for i in range(m)]
    if math.log2(n).is_integer():
        return pow2_slopes(n)
    closest = 2 ** math.floor(math.log2(n))
    return pow2_slopes(closest) + _alibi_slopes(2 * closest)[0::2][: n - closest]


def _norm_matmul_kernel(x_ref, g_ref, w_ref, ps_ref, o_ref, h_ref, *, norm_tiles):
    j = pl.program_id(1)

    @pl.when(j == 0)
    def _():
        x = x_ref[...]
        ms = jnp.mean(x * x, axis=-1, keepdims=True)
        h_ref[...] = (x * lax.rsqrt(ms + NORM_EPS) * g_ref[...]).astype(BF16)

    acc = jnp.dot(h_ref[...], w_ref[0], preferred_element_type=F32)
    tn = acc.shape[1]
    lo1, hi1, lo2, hi2 = norm_tiles
    normed = ((j >= lo1) & (j < hi1)) | ((j >= lo2) & (j < hi2))

    @pl.when(normed)
    def _():
        for c in range(tn // HEAD_DIM):
            sl = slice(c * HEAD_DIM, (c + 1) * HEAD_DIM)
            a = acc[:, sl]
            ms = jnp.mean(a * a, axis=-1, keepdims=True)
            o_ref[:, sl] = (a * lax.rsqrt(ms + NORM_EPS) * ps_ref[:, sl]).astype(o_ref.dtype)

    @pl.when(jnp.logical_not(normed))
    def _():
        o_ref[...] = (acc * ps_ref[...]).astype(o_ref.dtype)


def _norm_matmul(x, gain, w, layer, post_scale, norm_cols, *, tm, tn=512):
    m, d = x.shape
    n = w.shape[2]
    assert m % tm == 0 and n % tn == 0 and all(c % tn == 0 for c in norm_cols)
    norm_tiles = tuple(c // tn for c in norm_cols)
    return pl.pallas_call(
        functools.partial(_norm_matmul_kernel, norm_tiles=norm_tiles),
        out_shape=jax.ShapeDtypeStruct((m, n), BF16),
        grid=(m // tm, n // tn),
        in_specs=[
            pl.BlockSpec((tm, d), lambda i, j: (i, 0)),
            pl.BlockSpec((1, d), lambda i, j: (0, 0)),
            pl.BlockSpec((1, d, tn), lambda i, j: (layer, 0, j)),
            pl.BlockSpec((1, tn), lambda i, j: (0, j)),
        ],
        out_specs=pl.BlockSpec((tm, tn), lambda i, j: (i, j)),
        scratch_shapes=[pltpu.VMEM((tm, d), BF16)],
        compiler_params=_cparams(("parallel", "arbitrary")),
        name="norm_matmul",
    )(x, gain.reshape(1, d), w, post_scale.reshape(1, n))


def _win_attn_kernel(sink_ref, q_ref, kp_ref, kc_ref, kn_ref, vp_ref, vc_ref, vn_ref, o_ref, *, slopes):
    n = pl.program_id(1)
    nb = pl.num_programs(1)
    c = WINDOW
    qi = lax.broadcasted_iota(jnp.int32, (c, 3 * c), 0)
    sj = lax.broadcasted_iota(jnp.int32, (c, 3 * c), 1)
    dist = jnp.abs(qi + c - sj)
    key_lo = jnp.where(n > 0, 0, c)
    key_hi = jnp.where(n < nb - 1, 3 * c, 2 * c)
    valid = (dist <= WINDOW) & (sj >= key_lo) & (sj < key_hi)
    distf = dist.astype(F32)
    for h in range(ATTN_KV_HEADS):
        hs = slice(h * HEAD_DIM, (h + 1) * HEAD_DIM)
        kb = jnp.concatenate([kp_ref[:, hs], kc_ref[:, hs], kn_ref[:, hs]], axis=0)
        vb = jnp.concatenate([vp_ref[:, hs], vc_ref[:, hs], vn_ref[:, hs]], axis=0)
        for g in range(ATTN_GROUP):
            qh = h * ATTN_GROUP + g
            qs = slice(qh * HEAD_DIM, (qh + 1) * HEAD_DIM)
            s = lax.dot_general(q_ref[:, qs], kb, (((1,), (1,)), ((), ())),
                                preferred_element_type=F32)
            logits = jnp.where(valid, s - slopes[qh] * distf, NEG_INF)
            sink = sink_ref[qh]
            m = jnp.maximum(jnp.max(logits, axis=-1, keepdims=True), sink)
            p = jnp.exp(logits - m)
            denom = jnp.sum(p, axis=-1, keepdims=True) + jnp.exp(sink - m)
            o = jnp.dot(p.astype(BF16), vb, preferred_element_type=F32)
            o_ref[:, qs] = (o / denom).astype(o_ref.dtype)


def _win_attn(proj, sink, bsz, seq):
    c = WINDOW
    nb = seq // c
    kcol = ATTN_Q_W // ATTN_KV_W
    vcol = kcol + 1
    slopes = tuple(float(s) for s in np.array(_alibi_slopes(ATTN_Q_HEADS), np.float32))

    def kv_spec(col, shift):
        def imap(b, n):
            return (b * nb + jnp.clip(n + shift, 0, nb - 1), col)
        return pl.BlockSpec((c, ATTN_KV_W), imap)

    return pl.pallas_call(
        functools.partial(_win_attn_kernel, slopes=slopes),
        out_shape=jax.ShapeDtypeStruct((bsz * seq, ATTN_Q_W), BF16),
        grid=(bsz, nb),
        in_specs=[
            pl.BlockSpec(memory_space=pltpu.SMEM),
            pl.BlockSpec((c, ATTN_Q_W), lambda b, n: (b * nb + n, 0)),
            kv_spec(kcol, -1), kv_spec(kcol, 0), kv_spec(kcol, 1),
            kv_spec(vcol, -1), kv_spec(vcol, 0), kv_spec(vcol, 1),
        ],
        out_specs=pl.BlockSpec((c, ATTN_Q_W), lambda b, n: (b * nb + n, 0)),
        compiler_params=_cparams(("parallel", "arbitrary")),
        name="win_attn",
    )(sink, proj, proj, proj, proj, proj, proj, proj)


def _cross_attn_kernel(q_ref, kv_ref, o_ref):
    for h in range(CROSS_HEADS):
        hs = slice(h * HEAD_DIM, (h + 1) * HEAD_DIM)
        vs = slice(CROSS_W + h * HEAD_DIM, CROSS_W + (h + 1) * HEAD_DIM)
        s = lax.dot_general(q_ref[:, hs], kv_ref[:, hs], (((1,), (1,)), ((), ())),
                            preferred_element_type=F32)
        m = jnp.max(s, axis=-1, keepdims=True)
        p = jnp.exp(s - m)
        denom = jnp.sum(p, axis=-1, keepdims=True)
        o = jnp.dot(p.astype(BF16), kv_ref[:, vs], preferred_element_type=F32)
        o_ref[:, hs] = (o / denom).astype(o_ref.dtype)


def _cross_attn(proj, qcol, kvm, bsz, seq, mem_len, *, tq=512):
    nq = seq // tq
    return pl.pallas_call(
        _cross_attn_kernel,
        out_shape=jax.ShapeDtypeStruct((bsz * seq, CROSS_W), BF16),
        grid=(bsz, nq),
        in_specs=[
            pl.BlockSpec((tq, CROSS_W), lambda b, i: (b * nq + i, qcol)),
            pl.BlockSpec((mem_len, 2 * CROSS_W), lambda b, i: (b, 0)),
        ],
        out_specs=pl.BlockSpec((tq, CROSS_W), lambda b, i: (b * nq + i, 0)),
        compiler_params=_cparams(("parallel", "arbitrary")),
        name="cross_attn",
    )(proj, kvm)


def _log_sigmoid(x):
    return jnp.minimum(x, 0.0) - jnp.log1p(jnp.exp(-jnp.abs(x)))


def _retention_kernel(df_ref, db_ref, qf_ref, kf_ref, vf_ref, qb_ref, kb_ref, vb_ref,
                      of_ref, ob_ref, sf_ref, sb_ref):
    h = pl.program_id(1)
    n = pl.program_id(2)
    c = qf_ref.shape[0]

    @pl.when(n == 0)
    def _():
        sf_ref[...] = jnp.zeros_like(sf_ref)
        sb_ref[...] = jnp.zeros_like(sb_ref)

    ti = lax.broadcasted_iota(jnp.int32, (c, c), 0)
    si = lax.broadcasted_iota(jnp.int32, (c, c), 1)
    row = lax.broadcasted_iota(jnp.int32, (c, 1), 0).astype(F32)

    def one_direction(lg, q_ref, k_ref, v_ref, o_ref, s_ref, forward):
        delta = (ti - si) if forward else (si - ti)
        mask = (delta >= 0) if forward else (delta > 0)
        decay = jnp.where(mask, jnp.exp(lg * jnp.maximum(delta, 0).astype(F32)), 0.0)
        q = q_ref[...]
        k = k_ref[...]
        v = v_ref[...]
        scores = lax.dot_general(q, k, (((1,), (1,)), ((), ())), preferred_element_type=F32) * decay
        intra = jnp.dot(scores.astype(BF16), v, preferred_element_type=F32)
        xi = jnp.exp(lg * ((row + 1.0) if forward else (c - row)))
        zeta = jnp.exp(lg * ((c - 1.0 - row) if forward else row))
        state = s_ref[...]
        inter = jnp.dot((q.astype(F32) * xi).astype(BF16), state.astype(BF16),
                        preferred_element_type=F32)
        o_ref[...] = intra + inter
        kz = (k.astype(F32) * zeta).T.astype(BF16)
        s_ref[...] = jnp.exp(lg * c) * state + jnp.dot(kz, v, preferred_element_type=F32)

    lg_f = _log_sigmoid(jnp.full((1, 1), df_ref[h], F32))
    lg_b = _log_sigmoid(jnp.full((1, 1), db_ref[h], F32))
    one_direction(lg_f, qf_ref, kf_ref, vf_ref, of_ref, sf_ref, True)
    one_direction(lg_b, qb_ref, kb_ref, vb_ref, ob_ref, sb_ref, False)


def _retention(proj, decay_fwd, decay_bwd, bsz, seq):
    c = RET_CHUNK
    nc = seq // c
    kcol0 = RET_QK_W // RET_DK
    vcol0 = 2 * RET_QK_W // RET_DV

    def specs(rev):
        def chunk(b, n):
            return b * nc + ((nc - 1 - n) if rev else n)
        return [
            pl.BlockSpec((c, RET_DK), lambda b, h, n: (chunk(b, n), h)),
            pl.BlockSpec((c, RET_DK), lambda b, h, n: (chunk(b, n), kcol0 + h)),
            pl.BlockSpec((c, RET_DV), lambda b, h, n: (chunk(b, n), vcol0 + h)),
        ], pl.BlockSpec((c, RET_DV), lambda b, h, n: (chunk(b, n), h))

    in_f, out_f = specs(False)
    in_b, out_b = specs(True)
    smem = pl.BlockSpec(memory_space=pltpu.SMEM)
    out = jax.ShapeDtypeStruct((bsz * seq, RET_V_W), F32)
    return pl.pallas_call(
        _retention_kernel,
        out_shape=(out, out),
        grid=(bsz, RET_HEADS, nc),
        in_specs=[smem, smem] + in_f + in_b,
        out_specs=(out_f, out_b),
        scratch_shapes=[pltpu.VMEM((RET_DK, RET_DV), F32), pltpu.VMEM((RET_DK, RET_DV), F32)],
        compiler_params=_cparams(("parallel", "parallel", "arbitrary")),
        name="retention",
    )(decay_fwd, decay_bwd, proj, proj, proj, proj, proj, proj)


def _split_bf16(a):
    hi = a.astype(BF16)
    lo = (a - hi.astype(F32)).astype(BF16)
    return hi, lo


def _finish_out_proj(x_ref, mix, cross_ref, w_ref, g2_ref, wr_ref, xo_ref, h2_ref):
    mix_w = mix.shape[1]
    d = x_ref.shape[1]
    y = jnp.dot(mix, w_ref[0, :mix_w, :], preferred_element_type=F32)
    y = y + jnp.dot(cross_ref[...], w_ref[0, mix_w:, :], preferred_element_type=F32)
    x = x_ref[...] + y
    xo_ref[...] = x
    ms = jnp.mean(x * x, axis=-1, keepdims=True)
    h2 = x * lax.rsqrt(ms + NORM_EPS) * g2_ref[...]
    h2_ref[:, :d] = h2
    h_hi, h_lo = _split_bf16(h2)
    w_hi, w_lo = _split_bf16(wr_ref[...])
    both = jnp.dot(h_hi, jnp.concatenate([w_hi, w_lo], axis=1), preferred_element_type=F32)
    logits = both[:, :LANES] + both[:, LANES:] + jnp.dot(h_lo, w_hi, preferred_element_type=F32)
    lane = lax.broadcasted_iota(jnp.int32, logits.shape, 1)
    logits = jnp.where(lane < N_EXPERTS, logits, NEG_INF)
    m = jnp.max(logits, axis=-1, keepdims=True)
    p = jnp.exp(logits - m)
    h2_ref[:, d:] = p / jnp.sum(p, axis=-1, keepdims=True)


def _out_proj_attn_kernel(x_ref, mix_ref, cross_ref, w_ref, g2_ref, wr_ref, xo_ref, h2_ref):
    _finish_out_proj(x_ref, mix_ref[...], cross_ref, w_ref, g2_ref, wr_ref, xo_ref, h2_ref)


def _out_proj_ret_kernel(x_ref, fwd_ref, bwd_ref, gate_ref, og_ref, cross_ref, w_ref, g2_ref, wr_ref,
                         xo_ref, h2_ref):
    parts = []
    for h in range(RET_HEADS):
        hs = slice(h * RET_DV, (h + 1) * RET_DV)
        y = fwd_ref[:, hs] + bwd_ref[:, hs]
        ms = jnp.mean(y * y, axis=-1, keepdims=True)
        y = y * lax.rsqrt(ms + NORM_EPS) * og_ref[:, hs]
        g = gate_ref[:, hs].astype(F32)
        parts.append((g * jax.nn.sigmoid(g) * y).astype(BF16))
    mix = jnp.concatenate(parts, axis=-1)
    _finish_out_proj(x_ref, mix, cross_ref, w_ref, g2_ref, wr_ref, xo_ref, h2_ref)


def _out_proj(x, mix_inputs, cross, w_out, layer, g2, w_router, *, tm=512):
    m, d = x.shape
    row = lambda i: (i, 0)
    const = lambda i: (0, 0)
    tail_specs = [
        pl.BlockSpec((tm, CROSS_W), row),
        pl.BlockSpec((1, d, d), lambda i: (layer, 0, 0)),
        pl.BlockSpec((1, d), const),
        pl.BlockSpec((d, LANES), const),
    ]
    w_router = jnp.pad(w_router, ((0, 0), (0, LANES - N_EXPERTS)))
    tail_args = (cross, w_out, g2.reshape(1, d), w_router)
    if len(mix_inputs) == 1:
        kernel = _out_proj_attn_kernel
        specs = [pl.BlockSpec((tm, d), row), pl.BlockSpec((tm, ATTN_Q_W), row)] + tail_specs
        args = (x, mix_inputs[0]) + tail_args
    else:
        fwd, bwd, proj, gate_col, out_gain = mix_inputs
        kernel = _out_proj_ret_kernel
        specs = [
            pl.BlockSpec((tm, d), row),
            pl.BlockSpec((tm, RET_V_W), row),
            pl.BlockSpec((tm, RET_V_W), row),
            pl.BlockSpec((tm, RET_V_W), lambda i: (i, gate_col)),
            pl.BlockSpec((1, RET_V_W), const),
        ] + tail_specs
        args = (x, fwd, bwd, proj, out_gain.reshape(1, RET_V_W)) + tail_args
    return pl.pallas_call(
        kernel,
        out_shape=(jax.ShapeDtypeStruct((m, d), F32), jax.ShapeDtypeStruct((m, d + LANES), F32)),
        grid=(m // tm,),
        in_specs=specs,
        out_specs=(pl.BlockSpec((tm, d), row), pl.BlockSpec((tm, d + LANES), row)),
        compiler_params=_cparams(("parallel",)),
        name="out_proj",
    )(*args)


def _route_kernel(aff_ref, idx_ref, pos_ref, cum_ref, *, cap):
    aff = aff_ref[0]
    e, nt, _ = aff.shape
    rows = e * nt
    bits = pltpu.bitcast(aff, jnp.int32)
    capf = float(cap)

    def count(mask):
        c = jnp.sum(jnp.where(mask, 1.0, 0.0), axis=1, keepdims=True)
        return jnp.sum(c, axis=2, keepdims=True)

    def search(_, lohi):
        lo, hi = lohi
        mid = lo + ((hi - lo + 1) >> 1)
        ok = count(bits >= mid) >= capf
        return jnp.where(ok, mid, lo), jnp.where(ok, hi, mid - 1)

    lo0 = jnp.zeros((e, 1, 1), jnp.int32)
    hi0 = jnp.full((e, 1, 1), 0x7F800000, jnp.int32)
    thr, _ = lax.fori_loop(0, 31, search, (lo0, hi0))

    ri = lax.broadcasted_iota(jnp.int32, (LANES, LANES), 0)
    ci = lax.broadcasted_iota(jnp.int32, (LANES, LANES), 1)
    tri = jnp.where(ri <= ci, 1.0, 0.0).astype(BF16)
    shift = nt.bit_length() - 1
    rr = lax.broadcasted_iota(jnp.int32, (rows, rows), 0)
    cc = lax.broadcasted_iota(jnp.int32, (rows, rows), 1)
    earlier_tile = jnp.where(((rr >> shift) == (cc >> shift)) & (cc < rr), 1.0, 0.0).astype(BF16)

    def prefix(m3):
        m2 = m3.reshape(rows, LANES).astype(BF16)
        inc = jnp.dot(m2, tri, preferred_element_type=F32)
        tot = jnp.broadcast_to(inc[:, LANES - 1:LANES], (rows, LANES)).astype(BF16)
        offs = jnp.dot(earlier_tile, tot, preferred_element_type=F32)
        return (inc + offs).reshape(e, nt, LANES)

    gt = bits > thr
    eq = bits == thr
    need = capf - count(gt)
    eqf = jnp.where(eq, 1.0, 0.0)
    tie_rank = prefix(eqf) - eqf
    sel = gt | (eq & (tie_rank < need))
    self_ = jnp.where(sel, 1.0, 0.0)
    cum = prefix(self_)
    pos = (cum - self_).astype(jnp.int32)
    pos_ref[0] = jnp.where(sel, pos, -1 - pos)
    cum_ref[...] = cum

    nt_dims = (((1,), (1,)), ((), ()))
    kcol = lax.broadcasted_iota(jnp.int32, (cap, 1), 0).astype(F32)
    lane8 = lax.broadcasted_iota(jnp.int32, (8, LANES), 1)
    pick_last = jnp.where(lane8 == LANES - 1, 1.0, 0.0).astype(BF16)
    ones_nt = jnp.ones((8, nt), BF16)
    ones_ln = jnp.ones((8, LANES), BF16)

    def expert_list(ex, carry):
        cum_e = cum_ref[ex]
        hi = jnp.floor(cum_e * (1.0 / 32.0))
        lo = cum_e - 32.0 * hi
        hib, lob = hi.astype(BF16), lo.astype(BF16)
        tile_end = (32.0 * lax.dot_general(pick_last, hib, nt_dims, preferred_element_type=F32)
                    + lax.dot_general(pick_last, lob, nt_dims, preferred_element_type=F32))
        done = jnp.where(tile_end[0:1, :] <= kcol, 1.0, 0.0).astype(BF16)
        full = lax.dot_general(ones_nt, done, nt_dims, preferred_element_type=F32)
        step_hi = (pltpu.roll(hi, nt - 1, 0) - hi).astype(BF16)
        step_lo = (pltpu.roll(lo, nt - 1, 0) - lo).astype(BF16)
        sel_hi = jnp.dot(done, step_hi, preferred_element_type=F32) + hi[0:1, :]
        sel_lo = jnp.dot(done, step_lo, preferred_element_type=F32) + lo[0:1, :]
        inside = jnp.where(32.0 * sel_hi + sel_lo <= kcol, 1.0, 0.0).astype(BF16)
        part = lax.dot_general(ones_ln, inside, nt_dims, preferred_element_type=F32)
        idx_ref[0, pl.ds(pl.multiple_of(ex * 8, 8), 8), :] = (LANES * full + part).astype(jnp.int32)
        return carry

    lax.fori_loop(0, e, expert_list, 0)


def _route(aff_t, cap):
    bsz, e, s = aff_t.shape
    nt = s // LANES
    assert nt & (nt - 1) == 0
    blk = pl.BlockSpec((1, e, nt, LANES), lambda b: (b, 0, 0, 0))
    idx8, pos = pl.pallas_call(
        functools.partial(_route_kernel, cap=cap),
        out_shape=(jax.ShapeDtypeStruct((bsz, e * 8, cap), jnp.int32),
                   jax.ShapeDtypeStruct((bsz, e, nt, LANES), jnp.int32)),
        grid=(bsz,),
        in_specs=[blk],
        out_specs=(pl.BlockSpec((1, e * 8, cap), lambda b: (b, 0, 0)), blk),
        scratch_shapes=[pltpu.VMEM((e, nt, LANES), F32)],
        compiler_params=_cparams(("parallel",)),
        name="route",
    )(aff_t.reshape(bsz, e, nt, LANES))
    return idx8[:, ::8, :], pos.reshape(bsz, e, s)


def _expert_kernel(idx_ref, h_hbm, wg_ref, wu_ref, wd_ref, y_ref, xa, xb, sem, *, seq, cap, d):
    e = pl.program_id(0)
    b = pl.program_id(1)
    r = pl.program_id(2)
    n_e = pl.num_programs(0)
    n_b = pl.num_programs(1)
    rows = xa.shape[0]
    step = (e * n_b + b) * 2 + r
    n_steps = n_e * n_b * 2

    def gather(bb, ee, rr, buf, slot, unrolled):
        base = (bb * n_e + ee) * cap + rr * rows
        tok0 = bb * seq

        def start(i):
            tok = tok0 + idx_ref[base + i]
            pltpu.make_async_copy(h_hbm.at[pl.ds(tok, 1), :], buf.at[pl.ds(i, 1), :], sem.at[slot]).start()

        if unrolled:
            for i in range(rows):
                start(i)
        else:
            def body(i, carry):
                start(i)
                return carry
            lax.fori_loop(0, rows, body, 0)

    def wait(buf, slot):
        pltpu.make_async_copy(h_hbm.at[pl.ds(0, rows), :], buf, sem.at[slot]).wait()

    wrap_b = r == 1
    wrap_e = wrap_b & (b == n_b - 1)
    b1 = jnp.where(wrap_b, jnp.where(b == n_b - 1, 0, b + 1), b)
    e1 = jnp.where(wrap_e, jnp.where(e == n_e - 1, 0, e + 1), e)

    @pl.when(step == 0)
    def _():
        gather(b, e, r, xa, 0, False)

    def compute(cur, cur_slot, nxt, nxt_slot):
        gather(b1, e1, 1 - r, nxt, nxt_slot, True)
        wait(cur, cur_slot)
        x = cur[:, :d].astype(BF16)
        hg = jnp.dot(x, wg_ref[0, 0], preferred_element_type=F32)
        hu = jnp.dot(x, wu_ref[0, 0], preferred_element_type=F32)
        hid = (hg * jax.nn.sigmoid(hg) * hu).astype(BF16)
        y = jnp.dot(hid, wd_ref[0, 0], preferred_element_type=F32)
        lane = lax.broadcasted_iota(jnp.int32, (rows, LANES), 1)
        gate = jnp.sum(jnp.where(lane == e, cur[:, d:], 0.0), axis=-1, keepdims=True)
        y_ref[0, 0] = (y * gate).astype(y_ref.dtype)

    @pl.when(r == 0)
    def _():
        compute(xa, 0, xb, 1)

    @pl.when(r == 1)
    def _():
        compute(xb, 1, xa, 0)

    @pl.when(step == n_steps - 1)
    def _():
        wait(xa, 0)


def _experts(idx, h2, wg, wu, wd, layer, seq):
    bsz, e, cap = idx.shape
    d = h2.shape[1] - LANES
    ff = wg.shape[3]
    rows = cap // 2
    wmap = lambda ei, b, r, idx: (layer, ei, 0, 0)
    grid_spec = pltpu.PrefetchScalarGridSpec(
        num_scalar_prefetch=1,
        grid=(e, bsz, 2),
        in_specs=[
            pl.BlockSpec(memory_space=pl.ANY),
            pl.BlockSpec((1, 1, d, ff), wmap),
            pl.BlockSpec((1, 1, d, ff), wmap),
            pl.BlockSpec((1, 1, ff, d), wmap),
        ],
        out_specs=pl.BlockSpec((1, 1, rows, d), lambda ei, b, r, idx: (b, ei, r, 0)),
        scratch_shapes=[pltpu.VMEM((rows, d + LANES), F32), pltpu.VMEM((rows, d + LANES), F32),
                        pltpu.SemaphoreType.DMA((2,))],
    )
    return pl.pallas_call(
        functools.partial(_expert_kernel, seq=seq, cap=cap, d=d),
        out_shape=jax.ShapeDtypeStruct((bsz, e, cap, d), BF16),
        grid_spec=grid_spec,
        compiler_params=_cparams(("arbitrary", "arbitrary", "arbitrary")),
        name="experts",
    )(idx.reshape(-1), h2, wg, wu, wd)


def _combine_kernel(off_ref, x_ref, pos_ref, y_hbm, o_ref, ybuf, sem, *, cap, n_tiles):
    b = pl.program_id(0)
    i = pl.program_id(1)
    n_b = pl.num_programs(0)
    tt = x_ref.shape[0]
    n_exp = pos_ref.shape[1]
    ch = COMBINE_CHUNK
    per_tile = LANES // ch
    step = b * n_tiles + i
    slot = step % 2
    last_start = cap - ch

    def tile_rows(bb, ii):
        base = (bb * (n_tiles + 1) + ii) * n_exp
        first = [(off_ref[base + ex] // COMBINE_ALIGN) * COMBINE_ALIGN for ex in range(n_exp)]
        end = [off_ref[base + n_exp + ex] for ex in range(n_exp)]
        return first, end

    def chunk_start(lo):
        return pl.multiple_of(jnp.minimum(lo, last_start), COMBINE_ALIGN)

    def chunk_copy(bb, ex, lo, sl):
        return pltpu.make_async_copy(y_hbm.at[bb, ex, pl.ds(chunk_start(lo), ch), :],
                                     ybuf.at[sl, pl.ds(ex * ch, ch), :], sem.at[sl])

    def fetch(bb, los, sl):
        for ex in range(n_exp):
            chunk_copy(bb, ex, los[ex], sl).start()

    def wait(sl):
        for ex in range(n_exp):
            chunk_copy(0, ex, 0, sl).wait()

    lane = lax.broadcasted_iota(jnp.int32, (tt, LANES), 1)

    def onehot(los):
        blocks = []
        for j in range(n_exp // per_tile):
            target = jnp.full((tt, LANES), -1, jnp.int32)
            for u in range(per_tile):
                ex = j * per_tile + u
                p = pos_ref[:, ex:ex + 1]
                lo = los[ex]
                q = jnp.where((p >= lo) & (p < lo + ch), p - chunk_start(lo) + u * ch, -1)
                target = jnp.where((lane >= u * ch) & (lane < (u + 1) * ch), q, target)
            blocks.append(jnp.where(target == lane, 1.0, 0.0).astype(BF16))
        return jnp.concatenate(blocks, axis=1)

    first, end = tile_rows(b, i)

    @pl.when(step == 0)
    def _():
        fetch(b, first, slot)

    @pl.when(step < n_b * n_tiles - 1)
    def _():
        wrap = i == n_tiles - 1
        nxt, _ = tile_rows(jnp.where(wrap, b + 1, b), jnp.where(wrap, 0, i + 1))
        fetch(jnp.where(wrap, b + 1, b), nxt, 1 - slot)

    wait(slot)
    o_ref[...] = x_ref[...] + jnp.dot(onehot(first), ybuf[slot], preferred_element_type=F32)

    n_pass = functools.reduce(jnp.maximum, [(end[ex] - first[ex] + ch - 1) // ch for ex in range(n_exp)])

    def extra(c, carry):
        los = [first[ex] + c * ch for ex in range(n_exp)]
        fetch(b, los, slot)
        wait(slot)
        o_ref[...] += jnp.dot(onehot(los), ybuf[slot], preferred_element_type=F32)
        return carry

    lax.fori_loop(1, n_pass, extra, 0)


def _combine(x, pos, offsets, y, bsz, seq):
    d = x.shape[1]
    e, cap = y.shape[1], y.shape[2]
    tt = COMBINE_TOKENS
    n_tiles = seq // tt
    row = lambda b, i, off: (b * n_tiles + i, 0)
    grid_spec = pltpu.PrefetchScalarGridSpec(
        num_scalar_prefetch=1,
        grid=(bsz, n_tiles),
        in_specs=[
            pl.BlockSpec((tt, d), row),
            pl.BlockSpec((tt, e), row),
            pl.BlockSpec(memory_space=pl.ANY),
        ],
        out_specs=pl.BlockSpec((tt, d), row),
        scratch_shapes=[pltpu.VMEM((2, e * COMBINE_CHUNK, d), BF16), pltpu.SemaphoreType.DMA((2,))],
    )
    return pl.pallas_call(
        functools.partial(_combine_kernel, cap=cap, n_tiles=n_tiles),
        out_shape=jax.ShapeDtypeStruct(x.shape, F32),
        grid_spec=grid_spec,
        compiler_params=_cparams(("arbitrary", "arbitrary")),
        name="combine",
    )(offsets.reshape(-1), x, pos, y)


def _moe(x, h2, wg, wu, wd, layer, bsz, seq):
    d = x.shape[1]
    cap = CAPACITY_FACTOR * seq // N_EXPERTS
    aff_t = jnp.swapaxes(h2[:, d:d + N_EXPERTS].reshape(bsz, seq, N_EXPERTS), 1, 2)
    idx, pos_t = _route(aff_t, cap)
    y = _experts(idx, h2, wg, wu, wd, layer, seq)
    first = pos_t[:, :, ::COMBINE_TOKENS]
    offsets = jnp.concatenate(
        [jnp.where(first >= 0, first, -1 - first), jnp.full((bsz, N_EXPERTS, 1), cap, jnp.int32)], axis=-1)
    offsets = jnp.swapaxes(offsets, 1, 2)
    pos = jnp.swapaxes(pos_t, 1, 2).reshape(bsz * seq, N_EXPERTS)
    return _combine(x, pos, offsets, y, bsz, seq)


def kernel(x, mem, norm1_gain, w_in_attn, attn_q_gain, attn_k_gain, attn_sink, w_in_ret, ret_decay_fwd,
           ret_decay_bwd, ret_out_gain, mem_norm_gain, w_mem_kv, cross_q_gain, cross_k_gain, w_out,
           norm2_gain, w_router, w_gate, w_up, w_down):
    bsz, seq, d = x.shape
    mem_len = mem.shape[1]
    depth = norm1_gain.shape[0]
    xf = x.reshape(bsz * seq, d)
    memf = mem.reshape(bsz * mem_len, d)
    scale = HEAD_DIM ** -0.5
    ones = lambda n: jnp.ones((n,), F32)
    tile = lambda g, n: jnp.tile(g, n)
    w_in_attn, w_in_ret, w_mem_kv, w_out, w_gate, w_up, w_down = (
        w.astype(BF16) for w in (w_in_attn, w_in_ret, w_mem_kv, w_out, w_gate, w_up, w_down))
    for layer in range(depth):
        li = layer // 2
        kv_scale = jnp.concatenate([tile(cross_k_gain[layer], CROSS_HEADS), ones(CROSS_W)])
        kvm = _norm_matmul(memf, mem_norm_gain[layer], w_mem_kv, layer, kv_scale,
                           (0, CROSS_W, 0, 0), tm=bsz * mem_len)
        qc_scale = tile(cross_q_gain[layer] * scale, CROSS_HEADS)
        if layer % 2 == 0:
            post = jnp.concatenate([tile(attn_q_gain[li] * scale, ATTN_Q_HEADS),
                                    tile(attn_k_gain[li], ATTN_KV_HEADS), ones(ATTN_KV_W), qc_scale])
            q_end = ATTN_Q_W + ATTN_KV_W
            qc_start = q_end + ATTN_KV_W
            proj = _norm_matmul(xf, norm1_gain[layer], w_in_attn, li, post,
                                (0, q_end, qc_start, qc_start + CROSS_W), tm=1024)
            mix_inputs = (_win_attn(proj, attn_sink[li], bsz, seq),)
        else:
            post = jnp.concatenate([ones(RET_QK_W), jnp.full((RET_QK_W,), RET_DK ** -0.5, F32),
                                    ones(2 * RET_V_W), qc_scale])
            qc_start = 2 * RET_QK_W + 2 * RET_V_W
            proj = _norm_matmul(xf, norm1_gain[layer], w_in_ret, li, post,
                                (qc_start, qc_start + CROSS_W, 0, 0), tm=1024)
            fwd, bwd = _retention(proj, ret_decay_fwd[li], ret_decay_bwd[li], bsz, seq)
            gate_col = (2 * RET_QK_W + RET_V_W) // RET_V_W
            mix_inputs = (fwd, bwd, proj, gate_col, ret_out_gain[li])
        cross = _cross_attn(proj, qc_start // CROSS_W, kvm, bsz, seq, mem_len)
        xf, h2 = _out_proj(xf, mix_inputs, cross, w_out, layer, norm2_gain[layer], w_router[layer])
        xf = _moe(xf, h2, w_gate, w_up, w_down, layer, bsz, seq)
    return xf.reshape(bsz, seq, d)
```

```python
import functools
import math

import jax
import jax.numpy as jnp
import numpy as np
from jax import lax
from jax.experimental import pallas as pl
from jax.experimental.pallas import tpu as pltpu

F32 = jnp.float32
BF16 = jnp.bfloat16

HEAD_DIM = 128
ATTN_Q_HEADS = 12
ATTN_KV_HEADS = 4
ATTN_GROUP = ATTN_Q_HEADS // ATTN_KV_HEADS
WINDOW = 128
RET_HEADS = 6
RET_DK = 128
RET_DV = 256
CROSS_HEADS = 4
N_EXPERTS = 16
CAPACITY_FACTOR = 2
NORM_EPS = 1e-6
NEG_INF = -1e30

ATTN_Q_W = ATTN_Q_HEADS * HEAD_DIM
ATTN_KV_W = ATTN_KV_HEADS * HEAD_DIM
CROSS_W = CROSS_HEADS * HEAD_DIM
RET_QK_W = RET_HEADS * RET_DK
RET_V_W = RET_HEADS * RET_DV

LANES = 128
VMEM_LIMIT = 56 * 1024 * 1024
EXPERT_VMEM_LIMIT = 60 * 1024 * 1024

RET_CHUNK = 256
MOE_ROWS = 512
COMBINE_TOKENS = 256
COMBINE_CHUNK = 64
COMBINE_ALIGN = 16


def _cparams(semantics):
    return pltpu.CompilerParams(dimension_semantics=semantics, vmem_limit_bytes=VMEM_LIMIT)


def _alibi_slopes(n):
    def pow2_slopes(m):
        start = 2.0 ** (-(2.0 ** -(math.log2(m) - 3)))
        return [start ** (i + 1) for i in range(m)]
    if math.log2(n).is_integer():
        return pow2_slopes(n)
    closest = 2 ** math.floor(math.log2(n))
    return pow2_slopes(closest) + _alibi_slopes(2 * closest)[0::2][: n - closest]


def _norm_matmul_kernel(x_ref, g_ref, w_ref, ps_ref, o_ref, h_ref, *, norm_tiles):
    j = pl.program_id(1)

    @pl.when(j == 0)
    def _():
        x = x_ref[...]
        ms = jnp.mean(x * x, axis=-1, keepdims=True)
        h_ref[...] = (x * lax.rsqrt(ms + NORM_EPS) * g_ref[...]).astype(BF16)

    acc = jnp.dot(h_ref[...], w_ref[0], preferred_element_type=F32)
    tn = acc.shape[1]
    lo1, hi1, lo2, hi2 = norm_tiles
    normed = ((j >= lo1) & (j < hi1)) | ((j >= lo2) & (j < hi2))

    @pl.when(normed)
    def _():
        for c in range(tn // HEAD_DIM):
            sl = slice(c * HEAD_DIM, (c + 1) * HEAD_DIM)
            a = acc[:, sl]
            ms = jnp.mean(a * a, axis=-1, keepdims=True)
            o_ref[:, sl] = (a * lax.rsqrt(ms + NORM_EPS) * ps_ref[:, sl]).astype(o_ref.dtype)

    @pl.when(jnp.logical_not(normed))
    def _():
        o_ref[...] = (acc * ps_ref[...]).astype(o_ref.dtype)


def _norm_matmul(x, gain, w, layer, post_scale, norm_cols, *, tm, tn=512):
    m, d = x.shape
    n = w.shape[2]
    assert m % tm == 0 and n % tn == 0 and all(c % tn == 0 for c in norm_cols)
    norm_tiles = tuple(c // tn for c in norm_cols)
    return pl.pallas_call(
        functools.partial(_norm_matmul_kernel, norm_tiles=norm_tiles),
        out_shape=jax.ShapeDtypeStruct((m, n), BF16),
        grid=(m // tm, n // tn),
        in_specs=[
            pl.BlockSpec((tm, d), lambda i, j: (i, 0)),
            pl.BlockSpec((1, d), lambda i, j: (0, 0)),
            pl.BlockSpec((1, d, tn), lambda i, j: (layer, 0, j)),
            pl.BlockSpec((1, tn), lambda i, j: (0, j)),
        ],
        out_specs=pl.BlockSpec((tm, tn), lambda i, j: (i, j)),
        scratch_shapes=[pltpu.VMEM((tm, d), BF16)],
        compiler_params=_cparams(("parallel", "arbitrary")),
        name="norm_matmul",
    )(x, gain.reshape(1, d), w, post_scale.reshape(1, n))


def _win_attn_kernel(sink_ref, bias_ref, q_ref, kp_ref, kc_ref, kn_ref, vp_ref, vc_ref, vn_ref, o_ref):
    n = pl.program_id(1)
    nb = pl.num_programs(1)
    c = WINDOW
    col = lax.broadcasted_iota(jnp.int32, (1, 3 * c), 1)
    key_lo = jnp.where(n > 0, 0, c)
    key_hi = jnp.where(n < nb - 1, 3 * c, 2 * c)
    present = (col >= key_lo) & (col < key_hi)
    heads = range(ATTN_KV_HEADS)
    nt_dims = (((1,), (1,)), ((), ()))

    def band(p_ref, c_ref, n_ref, h):
        hs = slice(h * HEAD_DIM, (h + 1) * HEAD_DIM)
        return jnp.concatenate([p_ref[:, hs], c_ref[:, hs], n_ref[:, hs]], axis=0)

    def group_rows(h, make):
        return jnp.concatenate([make(h * ATTN_GROUP + g) for g in range(ATTN_GROUP)], axis=0)

    kb = [band(kp_ref, kc_ref, kn_ref, h) for h in heads]
    ones = jnp.ones((3 * c, HEAD_DIM), BF16)
    vb = [jnp.concatenate([band(vp_ref, vc_ref, vn_ref, h), ones], axis=1) for h in heads]
    q = [group_rows(h, lambda qh: q_ref[:, qh * HEAD_DIM:(qh + 1) * HEAD_DIM]) for h in heads]
    sink = [group_rows(h, lambda qh: jnp.full((c, 1), sink_ref[qh], F32)) for h in heads]
    s = [lax.dot_general(q[h], kb[h], nt_dims, preferred_element_type=F32) for h in heads]
    logits = [jnp.where(present, s[h] + bias_ref[h], NEG_INF) for h in heads]
    tile_max = [jnp.maximum(jnp.maximum(l[:, :c], l[:, c:2 * c]), l[:, 2 * c:]) for l in logits]
    m = [jnp.maximum(jnp.max(tile_max[h], axis=-1, keepdims=True), sink[h]) for h in heads]
    p = [jnp.exp(logits[h] - m[h]).astype(BF16) for h in heads]
    pv = [jnp.dot(p[h], vb[h], preferred_element_type=F32) for h in heads]
    o = [pv[h][:, :HEAD_DIM] / (pv[h][:, HEAD_DIM:] + jnp.exp(sink[h] - m[h])) for h in heads]
    for h in heads:
        for g in range(ATTN_GROUP):
            qh = h * ATTN_GROUP + g
            o_ref[:, qh * HEAD_DIM:(qh + 1) * HEAD_DIM] = o[h][g * c:(g + 1) * c].astype(o_ref.dtype)


def _alibi_band_bias():
    c = WINDOW
    dist = np.abs(np.arange(c)[:, None] + c - np.arange(3 * c)[None, :])
    slopes = np.array(_alibi_slopes(ATTN_Q_HEADS), np.float32)
    bias = np.where(dist <= WINDOW, -slopes[:, None, None] * dist.astype(np.float32)[None], np.float32(NEG_INF))
    return jnp.asarray(bias.astype(np.float32).reshape(ATTN_KV_HEADS, ATTN_GROUP * c, 3 * c))


def _win_attn(proj, sink, bsz, seq):
    c = WINDOW
    nb = seq // c
    kcol = ATTN_Q_W // ATTN_KV_W
    vcol = kcol + 1

    def kv_spec(col, shift):
        def imap(b, n):
            return (b * nb + jnp.clip(n + shift, 0, nb - 1), col)
        return pl.BlockSpec((c, ATTN_KV_W), imap)

    return pl.pallas_call(
        _win_attn_kernel,
        out_shape=jax.ShapeDtypeStruct((bsz * seq, ATTN_Q_W), BF16),
        grid=(bsz, nb),
        in_specs=[
            pl.BlockSpec(memory_space=pltpu.SMEM),
            pl.BlockSpec((ATTN_KV_HEADS, ATTN_GROUP * c, 3 * c), lambda b, n: (0, 0, 0)),
            pl.BlockSpec((c, ATTN_Q_W), lambda b, n: (b * nb + n, 0)),
            kv_spec(kcol, -1), kv_spec(kcol, 0), kv_spec(kcol, 1),
            kv_spec(vcol, -1), kv_spec(vcol, 0), kv_spec(vcol, 1),
        ],
        out_specs=pl.BlockSpec((c, ATTN_Q_W), lambda b, n: (b * nb + n, 0)),
        compiler_params=_cparams(("parallel", "arbitrary")),
        name="win_attn",
    )(sink, _alibi_band_bias(), proj, proj, proj, proj, proj, proj, proj)


def _cross_attn_kernel(q_ref, kv_ref, o_ref):
    for h in range(CROSS_HEADS):
        hs = slice(h * HEAD_DIM, (h + 1) * HEAD_DIM)
        vs = slice(CROSS_W + h * HEAD_DIM, CROSS_W + (h + 1) * HEAD_DIM)
        s = lax.dot_general(q_ref[:, hs], kv_ref[:, hs], (((1,), (1,)), ((), ())),
                            preferred_element_type=F32)
        m = jnp.max(s, axis=-1, keepdims=True)
        p = jnp.exp(s - m)
        denom = jnp.sum(p, axis=-1, keepdims=True)
        o = jnp.dot(p.astype(BF16), kv_ref[:, vs], preferred_element_type=F32)
        o_ref[:, hs] = (o / denom).astype(o_ref.dtype)


def _cross_attn(proj, qcol, kvm, bsz, seq, mem_len, *, tq=512):
    nq = seq // tq
    return pl.pallas_call(
        _cross_attn_kernel,
        out_shape=jax.ShapeDtypeStruct((bsz * seq, CROSS_W), BF16),
        grid=(bsz, nq),
        in_specs=[
            pl.BlockSpec((tq, CROSS_W), lambda b, i: (b * nq + i, qcol)),
            pl.BlockSpec((mem_len, 2 * CROSS_W), lambda b, i: (b, 0)),
        ],
        out_specs=pl.BlockSpec((tq, CROSS_W), lambda b, i: (b * nq + i, 0)),
        compiler_params=_cparams(("parallel", "arbitrary")),
        name="cross_attn",
    )(proj, kvm)


def _log_sigmoid(x):
    return jnp.minimum(x, 0.0) - jnp.log1p(jnp.exp(-jnp.abs(x)))


def _retention_kernel(df_ref, db_ref, qf_ref, kf_ref, vf_ref, qb_ref, kb_ref, vb_ref,
                      of_ref, ob_ref, state_ref, decay_ref):
    n = pl.program_id(1)
    c = qf_ref.shape[0]
    nt_dims = (((1,), (1,)), ((), ()))
    row = lax.broadcasted_iota(jnp.int32, (c, 1), 0).astype(F32)
    chains = [(h, fwd) for h in range(RET_HEADS) for fwd in (True, False)]
    lg = [_log_sigmoid(jnp.full((1, 1), (df_ref if fwd else db_ref)[h], F32)) for h, fwd in chains]

    @pl.when(n == 0)
    def _():
        state_ref[...] = jnp.zeros_like(state_ref)
        ti = lax.broadcasted_iota(jnp.int32, (c, c), 0)
        si = lax.broadcasted_iota(jnp.int32, (c, c), 1)
        for i, (h, fwd) in enumerate(chains):
            delta = (ti - si) if fwd else (si - ti)
            mask = (delta >= 0) if fwd else (delta > 0)
            decay_ref[i] = jnp.where(mask, jnp.exp(lg[i] * jnp.maximum(delta, 0).astype(F32)), 0.0)

    def qkv(i):
        h, fwd = chains[i]
        q_ref, k_ref, v_ref = (qf_ref, kf_ref, vf_ref) if fwd else (qb_ref, kb_ref, vb_ref)
        return (q_ref[:, h * RET_DK:(h + 1) * RET_DK], k_ref[:, h * RET_DK:(h + 1) * RET_DK],
                v_ref[:, h * RET_DV:(h + 1) * RET_DV])

    ids = range(len(chains))
    scores = [(lax.dot_general(qkv(i)[0], qkv(i)[1], nt_dims, preferred_element_type=F32)
               * decay_ref[i]).astype(BF16) for i in ids]
    xi = [jnp.exp(lg[i] * ((row + 1.0) if chains[i][1] else (c - row))) for i in ids]
    zeta = [jnp.exp(lg[i] * ((c - 1.0 - row) if chains[i][1] else row)) for i in ids]
    qx = [(qkv(i)[0].astype(F32) * xi[i]).astype(BF16) for i in ids]
    kz = [(qkv(i)[1].astype(F32) * zeta[i]).T.astype(BF16) for i in ids]
    state = [state_ref[i] for i in ids]
    out = [jnp.dot(scores[i], qkv(i)[2], preferred_element_type=F32)
           + jnp.dot(qx[i], state[i].astype(BF16), preferred_element_type=F32) for i in ids]
    for i, (h, fwd) in enumerate(chains):
        o_ref = of_ref if fwd else ob_ref
        o_ref[:, h * RET_DV:(h + 1) * RET_DV] = out[i].astype(o_ref.dtype)
        state_ref[i] = jnp.exp(lg[i] * c) * state[i] + jnp.dot(kz[i], qkv(i)[2], preferred_element_type=F32)


def _retention(proj, decay_fwd, decay_bwd, bsz, seq):
    c = RET_CHUNK
    nc = seq // c

    def specs(rev):
        def chunk(b, n):
            return b * nc + ((nc - 1 - n) if rev else n)
        return [
            pl.BlockSpec((c, RET_QK_W), lambda b, n: (chunk(b, n), 0)),
            pl.BlockSpec((c, RET_QK_W), lambda b, n: (chunk(b, n), 1)),
            pl.BlockSpec((c, RET_V_W), lambda b, n: (chunk(b, n), 2 * RET_QK_W // RET_V_W)),
        ], pl.BlockSpec((c, RET_V_W), lambda b, n: (chunk(b, n), 0))

    in_f, out_f = specs(False)
    in_b, out_b = specs(True)
    smem = pl.BlockSpec(memory_space=pltpu.SMEM)
    out = jax.ShapeDtypeStruct((bsz * seq, RET_V_W), BF16)
    n_chains = 2 * RET_HEADS
    return pl.pallas_call(
        _retention_kernel,
        out_shape=(out, out),
        grid=(bsz, nc),
        in_specs=[smem, smem] + in_f + in_b,
        out_specs=(out_f, out_b),
        scratch_shapes=[pltpu.VMEM((n_chains, RET_DK, RET_DV), F32), pltpu.VMEM((n_chains, c, c), F32)],
        compiler_params=_cparams(("parallel", "arbitrary")),
        name="retention",
    )(decay_fwd, decay_bwd, proj, proj, proj, proj, proj, proj)


def _split_bf16(a):
    hi = a.astype(BF16)
    lo = (a - hi.astype(F32)).astype(BF16)
    return hi, lo


def _finish_out_proj(x_ref, mix, cross_ref, w_ref, g2_ref, wr_ref, xo_ref, h2_ref):
    mix_w = mix.shape[1]
    d = x_ref.shape[1]
    y = jnp.dot(mix, w_ref[0, :mix_w, :], preferred_element_type=F32)
    y = y + jnp.dot(cross_ref[...], w_ref[0, mix_w:, :], preferred_element_type=F32)
    x = x_ref[...] + y
    xo_ref[...] = x
    ms = jnp.mean(x * x, axis=-1, keepdims=True)
    h2 = x * lax.rsqrt(ms + NORM_EPS) * g2_ref[...]
    h2_ref[:, :d] = h2
    h_hi, h_lo = _split_bf16(h2)
    w_hi, w_lo = _split_bf16(wr_ref[...])
    both = jnp.dot(h_hi, jnp.concatenate([w_hi, w_lo], axis=1), preferred_element_type=F32)
    logits = both[:, :LANES] + both[:, LANES:] + jnp.dot(h_lo, w_hi, preferred_element_type=F32)
    lane = lax.broadcasted_iota(jnp.int32, logits.shape, 1)
    logits = jnp.where(lane < N_EXPERTS, logits, NEG_INF)
    m = jnp.max(logits, axis=-1, keepdims=True)
    p = jnp.exp(logits - m)
    h2_ref[:, d:] = p / jnp.sum(p, axis=-1, keepdims=True)


def _out_proj_attn_kernel(x_ref, mix_ref, cross_ref, w_ref, g2_ref, wr_ref, xo_ref, h2_ref):
    _finish_out_proj(x_ref, mix_ref[...], cross_ref, w_ref, g2_ref, wr_ref, xo_ref, h2_ref)


def _out_proj_ret_kernel(x_ref, fwd_ref, bwd_ref, gate_ref, og_ref, cross_ref, w_ref, g2_ref, wr_ref,
                         xo_ref, h2_ref):
    parts = []
    for h in range(RET_HEADS):
        hs = slice(h * RET_DV, (h + 1) * RET_DV)
        y = fwd_ref[:, hs].astype(F32) + bwd_ref[:, hs].astype(F32)
        ms = jnp.mean(y * y, axis=-1, keepdims=True)
        y = y * lax.rsqrt(ms + NORM_EPS) * og_ref[:, hs]
        g = gate_ref[:, hs].astype(F32)
        parts.append((g * jax.nn.sigmoid(g) * y).astype(BF16))
    mix = jnp.concatenate(parts, axis=-1)
    _finish_out_proj(x_ref, mix, cross_ref, w_ref, g2_ref, wr_ref, xo_ref, h2_ref)


def _out_proj(x, mix_inputs, cross, w_out, layer, g2, w_router, *, tm=512):
    m, d = x.shape
    row = lambda i: (i, 0)
    const = lambda i: (0, 0)
    tail_specs = [
        pl.BlockSpec((tm, CROSS_W), row),
        pl.BlockSpec((1, d, d), lambda i: (layer, 0, 0)),
        pl.BlockSpec((1, d), const),
        pl.BlockSpec((d, LANES), const),
    ]
    w_router = jnp.pad(w_router, ((0, 0), (0, LANES - N_EXPERTS)))
    tail_args = (cross, w_out, g2.reshape(1, d), w_router)
    if len(mix_inputs) == 1:
        kernel = _out_proj_attn_kernel
        specs = [pl.BlockSpec((tm, d), row), pl.BlockSpec((tm, ATTN_Q_W), row)] + tail_specs
        args = (x, mix_inputs[0]) + tail_args
    else:
        fwd, bwd, proj, gate_col, out_gain = mix_inputs
        kernel = _out_proj_ret_kernel
        specs = [
            pl.BlockSpec((tm, d), row),
            pl.BlockSpec((tm, RET_V_W), row),
            pl.BlockSpec((tm, RET_V_W), row),
            pl.BlockSpec((tm, RET_V_W), lambda i: (i, gate_col)),
            pl.BlockSpec((1, RET_V_W), const),
        ] + tail_specs
        args = (x, fwd, bwd, proj, out_gain.reshape(1, RET_V_W)) + tail_args
    return pl.pallas_call(
        kernel,
        out_shape=(jax.ShapeDtypeStruct((m, d), F32), jax.ShapeDtypeStruct((m, d + LANES), F32)),
        grid=(m // tm,),
        in_specs=specs,
        out_specs=(pl.BlockSpec((tm, d), row), pl.BlockSpec((tm, d + LANES), row)),
        compiler_params=_cparams(("parallel",)),
        name="out_proj",
    )(*args)


def _route_kernel(aff_ref, idx_ref, pos_ref, cum_ref, *, cap):
    aff = aff_ref[0]
    e, nt, _ = aff.shape
    rows = e * nt
    bits = pltpu.bitcast(aff, jnp.int32)
    capf = float(cap)

    def count(mask):
        c = jnp.sum(jnp.where(mask, 1.0, 0.0), axis=1, keepdims=True)
        return jnp.sum(c, axis=2, keepdims=True)

    def search(_, lohi):
        lo, hi = lohi
        mid = lo + ((hi - lo + 1) >> 1)
        ok = count(bits >= mid) >= capf
        return jnp.where(ok, mid, lo), jnp.where(ok, hi, mid - 1)

    lo0 = jnp.zeros((e, 1, 1), jnp.int32)
    hi0 = jnp.full((e, 1, 1), 0x7F800000, jnp.int32)
    thr, _ = lax.fori_loop(0, 31, search, (lo0, hi0))

    ri = lax.broadcasted_iota(jnp.int32, (LANES, LANES), 0)
    ci = lax.broadcasted_iota(jnp.int32, (LANES, LANES), 1)
    tri = jnp.where(ri <= ci, 1.0, 0.0).astype(BF16)
    shift = nt.bit_length() - 1
    rr = lax.broadcasted_iota(jnp.int32, (rows, rows), 0)
    cc = lax.broadcasted_iota(jnp.int32, (rows, rows), 1)
    earlier_tile = jnp.where(((rr >> shift) == (cc >> shift)) & (cc < rr), 1.0, 0.0).astype(BF16)

    def prefix(m3):
        m2 = m3.reshape(rows, LANES).astype(BF16)
        inc = jnp.dot(m2, tri, preferred_element_type=F32)
        tot = jnp.broadcast_to(inc[:, LANES - 1:LANES], (rows, LANES)).astype(BF16)
        offs = jnp.dot(earlier_tile, tot, preferred_element_type=F32)
        return (inc + offs).reshape(e, nt, LANES)

    gt = bits > thr
    eq = bits == thr
    need = capf - count(gt)
    eqf = jnp.where(eq, 1.0, 0.0)
    tie_rank = prefix(eqf) - eqf
    sel = gt | (eq & (tie_rank < need))
    self_ = jnp.where(sel, 1.0, 0.0)
    cum = prefix(self_)
    pos = (cum - self_).astype(jnp.int32)
    pos_ref[0] = jnp.where(sel, pos, -1 - pos)
    cum_ref[...] = cum

    nt_dims = (((1,), (1,)), ((), ()))
    kcol = lax.broadcasted_iota(jnp.int32, (cap, 1), 0).astype(F32)
    lane8 = lax.broadcasted_iota(jnp.int32, (8, LANES), 1)
    pick_last = jnp.where(lane8 == LANES - 1, 1.0, 0.0).astype(BF16)
    ones_nt = jnp.ones((8, nt), BF16)
    ones_ln = jnp.ones((8, LANES), BF16)

    def expert_list(ex, carry):
        cum_e = cum_ref[ex]
        hi = jnp.floor(cum_e * (1.0 / 32.0))
        lo = cum_e - 32.0 * hi
        hib, lob = hi.astype(BF16), lo.astype(BF16)
        tile_end = (32.0 * lax.dot_general(pick_last, hib, nt_dims, preferred_element_type=F32)
                    + lax.dot_general(pick_last, lob, nt_dims, preferred_element_type=F32))
        done = jnp.where(tile_end[0:1, :] <= kcol, 1.0, 0.0).astype(BF16)
        full = lax.dot_general(ones_nt, done, nt_dims, preferred_element_type=F32)
        step_hi = (pltpu.roll(hi, nt - 1, 0) - hi).astype(BF16)
        step_lo = (pltpu.roll(lo, nt - 1, 0) - lo).astype(BF16)
        sel_hi = jnp.dot(done, step_hi, preferred_element_type=F32) + hi[0:1, :]
        sel_lo = jnp.dot(done, step_lo, preferred_element_type=F32) + lo[0:1, :]
        inside = jnp.where(32.0 * sel_hi + sel_lo <= kcol, 1.0, 0.0).astype(BF16)
        part = lax.dot_general(ones_ln, inside, nt_dims, preferred_element_type=F32)
        idx_ref[0, pl.ds(pl.multiple_of(ex * 8, 8), 8), :] = (LANES * full + part).astype(jnp.int32)
        return carry

    lax.fori_loop(0, e, expert_list, 0)


def _route(aff_t, cap):
    bsz, e, s = aff_t.shape
    nt = s // LANES
    assert nt & (nt - 1) == 0
    blk = pl.BlockSpec((1, e, nt, LANES), lambda b: (b, 0, 0, 0))
    idx8, pos = pl.pallas_call(
        functools.partial(_route_kernel, cap=cap),
        out_shape=(jax.ShapeDtypeStruct((bsz, e * 8, cap), jnp.int32),
                   jax.ShapeDtypeStruct((bsz, e, nt, LANES), jnp.int32)),
        grid=(bsz,),
        in_specs=[blk],
        out_specs=(pl.BlockSpec((1, e * 8, cap), lambda b: (b, 0, 0)), blk),
        scratch_shapes=[pltpu.VMEM((e, nt, LANES), F32)],
        compiler_params=_cparams(("parallel",)),
        name="route",
    )(aff_t.reshape(bsz, e, nt, LANES))
    return idx8[:, ::8, :], pos.reshape(bsz, e, s)


def _expert_kernel(idx_ref, h_hbm, wg_hbm, wu_hbm, wd_hbm, y_ref, xa, xb, sem, wg_s, wu_s, wd_s, stg_a, stg_d,
                   wsem, *, layer, cap, d):
    e = pl.program_id(0)
    b = pl.program_id(1)
    r = pl.program_id(2)
    n_e = pl.num_programs(0)
    n_b = pl.num_programs(1)
    rows = xa.shape[0]
    step = (e * n_b + b) * 2 + r
    n_steps = n_e * n_b * 2
    sub = b * 2 + r
    n_sub = n_b * 2
    ca = stg_a.shape[1]
    cd = stg_d.shape[1]

    def pieces(s):
        out = []
        for j in range(2):
            ra = pl.multiple_of((2 * s + j) * ca, ca)
            rd = pl.multiple_of((2 * s + j) * cd, cd)
            out.append((wg_hbm, wg_s, ra, ca, stg_a.at[j]))
            out.append((wu_hbm, wu_s, ra, ca, stg_a.at[2 + j]))
            out.append((wd_hbm, wd_s, rd, cd, stg_d.at[j]))
        return out

    def stage_copies(ee, s):
        return [pltpu.make_async_copy(src.at[layer, ee, pl.ds(r0, n), :], stg, wsem.at[0])
                for src, _, r0, n, stg in pieces(s)]

    def convert(dst_set, s):
        for _, dst, r0, n, stg in pieces(s):
            dst[dst_set, pl.ds(r0, n), :] = stg[...].astype(BF16)

    next_expert = lambda ee: jnp.where(ee == n_e - 1, 0, ee + 1)
    cur_set = e % 2

    def gather(bb, ee, rr, buf, slot, unrolled):
        base = (bb * n_e + ee) * cap + rr * rows

        def start(i):
            pltpu.make_async_copy(h_hbm.at[pl.ds(idx_ref[base + i], 1), :], buf.at[pl.ds(i, 1), :],
                                  sem.at[slot]).start()

        if unrolled:
            for i in range(rows):
                start(i)
        else:
            def body(i, carry):
                start(i)
                return carry
            lax.fori_loop(0, rows, body, 0)

    def wait(buf, slot):
        pltpu.make_async_copy(h_hbm.at[pl.ds(0, rows), :], buf, sem.at[slot]).wait()

    wrap_b = r == 1
    wrap_e = wrap_b & (b == n_b - 1)
    b1 = jnp.where(wrap_b, jnp.where(b == n_b - 1, 0, b + 1), b)
    e1 = jnp.where(wrap_e, jnp.where(e == n_e - 1, 0, e + 1), e)

    @pl.when(step == 0)
    def _():
        gather(b, e, r, xa, 0, False)
        for s in range(n_sub):
            for cp in stage_copies(0, s):
                cp.start()
            for cp in stage_copies(0, s):
                cp.wait()
            convert(0, s)
        for cp in stage_copies(next_expert(e), sub):
            cp.start()

    def compute(cur, cur_slot, nxt, nxt_slot):
        gather(b1, e1, 1 - r, nxt, nxt_slot, True)
        wait(cur, cur_slot)
        x = cur[:, :d].astype(BF16)
        hg = jnp.dot(x, wg_s[cur_set], preferred_element_type=F32)
        for cp in stage_copies(next_expert(e), sub):
            cp.wait()
        convert(1 - cur_set, sub)
        hu = jnp.dot(x, wu_s[cur_set], preferred_element_type=F32)
        hid = (hg * jax.nn.sigmoid(hg) * hu).astype(BF16)
        lane = lax.broadcasted_iota(jnp.int32, (rows, LANES), 1)
        gate = jnp.sum(jnp.where(lane == e, cur[:, d:], 0.0), axis=-1, keepdims=True)
        half = d // 2
        for c in range(2):
            cols = slice(c * half, (c + 1) * half)
            y = jnp.dot(hid, wd_s[cur_set, :, cols], preferred_element_type=F32)
            y_ref[0, 0, :, cols] = (y * gate).astype(y_ref.dtype)

        @pl.when(step < n_steps - 1)
        def _():
            last_sub = sub == n_sub - 1
            e_next = jnp.where(last_sub, e + 1, e)
            for cp in stage_copies(next_expert(e_next), jnp.where(last_sub, 0, sub + 1)):
                cp.start()

    @pl.when(r == 0)
    def _():
        compute(xa, 0, xb, 1)

    @pl.when(r == 1)
    def _():
        compute(xb, 1, xa, 0)

    @pl.when(step == n_steps - 1)
    def _():
        wait(xa, 0)


def _experts(idx, h2, wg, wu, wd, layer):
    bsz, e, cap = idx.shape
    d = h2.shape[1] - LANES
    ff = wg.shape[3]
    rows = cap // 2
    n_pieces = 2 * 2 * bsz
    any_space = pl.BlockSpec(memory_space=pl.ANY)
    grid_spec = pltpu.PrefetchScalarGridSpec(
        num_scalar_prefetch=1,
        grid=(e, bsz, 2),
        in_specs=[any_space, any_space, any_space, any_space],
        out_specs=pl.BlockSpec((1, 1, rows, d), lambda ei, b, r, idx: (b, ei, r, 0)),
        scratch_shapes=[pltpu.VMEM((rows, d + LANES), F32), pltpu.VMEM((rows, d + LANES), F32),
                        pltpu.SemaphoreType.DMA((2,)),
                        pltpu.VMEM((2, d, ff), BF16), pltpu.VMEM((2, d, ff), BF16), pltpu.VMEM((2, ff, d), BF16),
                        pltpu.VMEM((4, d // n_pieces, ff), F32), pltpu.VMEM((2, ff // n_pieces, d), F32),
                        pltpu.SemaphoreType.DMA((1,))],
    )
    return pl.pallas_call(
        functools.partial(_expert_kernel, layer=layer, cap=cap, d=d),
        out_shape=jax.ShapeDtypeStruct((bsz, e, cap, d), BF16),
        grid_spec=grid_spec,
        compiler_params=pltpu.CompilerParams(dimension_semantics=("arbitrary", "arbitrary", "arbitrary"),
                                             vmem_limit_bytes=EXPERT_VMEM_LIMIT),
        name="experts",
    )(idx.reshape(-1), h2, wg, wu, wd)


def _combine_kernel(off_ref, x_ref, pos_ref, y_hbm, o_ref, ybuf, sem, *, cap, n_tiles):
    b = pl.program_id(0)
    i = pl.program_id(1)
    n_b = pl.num_programs(0)
    tt = x_ref.shape[0]
    n_exp = pos_ref.shape[1]
    ch = COMBINE_CHUNK
    per_tile = LANES // ch
    step = b * n_tiles + i
    slot = step % 2
    last_start = cap - ch

    def tile_rows(bb, ii):
        base = (bb * (n_tiles + 1) + ii) * n_exp
        first = [(off_ref[base + ex] // COMBINE_ALIGN) * COMBINE_ALIGN for ex in range(n_exp)]
        end = [off_ref[base + n_exp + ex] for ex in range(n_exp)]
        return first, end

    def chunk_start(lo):
        return pl.multiple_of(jnp.minimum(lo, last_start), COMBINE_ALIGN)

    def chunk_copy(bb, ex, lo, sl):
        return pltpu.make_async_copy(y_hbm.at[bb, ex, pl.ds(chunk_start(lo), ch), :],
                                     ybuf.at[sl, pl.ds(ex * ch, ch), :], sem.at[sl])

    def fetch(bb, los, sl):
        for ex in range(n_exp):
            chunk_copy(bb, ex, los[ex], sl).start()

    def wait(sl):
        for ex in range(n_exp):
            chunk_copy(0, ex, 0, sl).wait()

    lane = lax.broadcasted_iota(jnp.int32, (tt, LANES), 1)

    def onehot(los):
        blocks = []
        for j in range(n_exp // per_tile):
            target = jnp.full((tt, LANES), -1, jnp.int32)
            for u in range(per_tile):
                ex = j * per_tile + u
                p = pos_ref[:, ex:ex + 1]
                lo = los[ex]
                q = jnp.where((p >= lo) & (p < lo + ch), p - chunk_start(lo) + u * ch, -1)
                target = jnp.where((lane >= u * ch) & (lane < (u + 1) * ch), q, target)
            blocks.append(jnp.where(target == lane, 1.0, 0.0).astype(BF16))
        return jnp.concatenate(blocks, axis=1)

    first, end = tile_rows(b, i)

    @pl.when(step == 0)
    def _():
        fetch(b, first, slot)

    @pl.when(step < n_b * n_tiles - 1)
    def _():
        wrap = i == n_tiles - 1
        nxt, _ = tile_rows(jnp.where(wrap, b + 1, b), jnp.where(wrap, 0, i + 1))
        fetch(jnp.where(wrap, b + 1, b), nxt, 1 - slot)

    wait(slot)
    o_ref[...] = x_ref[...] + jnp.dot(onehot(first), ybuf[slot], preferred_element_type=F32)

    n_pass = functools.reduce(jnp.maximum, [(end[ex] - first[ex] + ch - 1) // ch for ex in range(n_exp)])

    def extra(c, carry):
        los = [first[ex] + c * ch for ex in range(n_exp)]
        fetch(b, los, slot)
        wait(slot)
        o_ref[...] += jnp.dot(onehot(los), ybuf[slot], preferred_element_type=F32)
        return carry

    lax.fori_loop(1, n_pass, extra, 0)


def _combine(x, pos, offsets, y, bsz, seq):
    d = x.shape[1]
    e, cap = y.shape[1], y.shape[2]
    tt = COMBINE_TOKENS
    n_tiles = seq // tt
    row = lambda b, i, off: (b * n_tiles + i, 0)
    grid_spec = pltpu.PrefetchScalarGridSpec(
        num_scalar_prefetch=1,
        grid=(bsz, n_tiles),
        in_specs=[
            pl.BlockSpec((tt, d), row),
            pl.BlockSpec((tt, e), row),
            pl.BlockSpec(memory_space=pl.ANY),
        ],
        out_specs=pl.BlockSpec((tt, d), row),
        scratch_shapes=[pltpu.VMEM((2, e * COMBINE_CHUNK, d), BF16), pltpu.SemaphoreType.DMA((2,))],
    )
    return pl.pallas_call(
        functools.partial(_combine_kernel, cap=cap, n_tiles=n_tiles),
        out_shape=jax.ShapeDtypeStruct(x.shape, F32),
        grid_spec=grid_spec,
        compiler_params=_cparams(("arbitrary", "arbitrary")),
        name="combine",
    )(offsets.reshape(-1), x, pos, y)


def _moe(x, h2, wg, wu, wd, layer, bsz, seq):
    d = x.shape[1]
    cap = CAPACITY_FACTOR * seq // N_EXPERTS
    aff_t = jnp.swapaxes(h2[:, d:d + N_EXPERTS].reshape(bsz, seq, N_EXPERTS), 1, 2)
    idx, pos_t = _route(aff_t, cap)
    rows_of = idx + (jnp.arange(bsz, dtype=jnp.int32) * seq)[:, None, None]
    y = _experts(rows_of, h2, wg, wu, wd, layer)
    first = pos_t[:, :, ::COMBINE_TOKENS]
    offsets = jnp.concatenate(
        [jnp.where(first >= 0, first, -1 - first), jnp.full((bsz, N_EXPERTS, 1), cap, jnp.int32)], axis=-1)
    offsets = jnp.swapaxes(offsets, 1, 2)
    pos = jnp.swapaxes(pos_t, 1, 2).reshape(bsz * seq, N_EXPERTS)
    return _combine(x, pos, offsets, y, bsz, seq)


def kernel(x, mem, norm1_gain, w_in_attn, attn_q_gain, attn_k_gain, attn_sink, w_in_ret, ret_decay_fwd,
           ret_decay_bwd, ret_out_gain, mem_norm_gain, w_mem_kv, cross_q_gain, cross_k_gain, w_out,
           norm2_gain, w_router, w_gate, w_up, w_down):
    bsz, seq, d = x.shape
    mem_len = mem.shape[1]
    depth = norm1_gain.shape[0]
    xf = x.reshape(bsz * seq, d)
    memf = mem.reshape(bsz * mem_len, d)
    scale = HEAD_DIM ** -0.5
    ones = lambda n: jnp.ones((n,), F32)
    tile = lambda g, n: jnp.tile(g, n)
    w_in_attn, w_in_ret, w_mem_kv, w_out = (w.astype(BF16) for w in (w_in_attn, w_in_ret, w_mem_kv, w_out))
    for layer in range(depth):
        li = layer // 2
        kv_scale = jnp.concatenate([tile(cross_k_gain[layer], CROSS_HEADS), ones(CROSS_W)])
        kvm = _norm_matmul(memf, mem_norm_gain[layer], w_mem_kv, layer, kv_scale,
                           (0, CROSS_W, 0, 0), tm=bsz * mem_len)
        qc_scale = tile(cross_q_gain[layer] * scale, CROSS_HEADS)
        if layer % 2 == 0:
            post = jnp.concatenate([tile(attn_q_gain[li] * scale, ATTN_Q_HEADS),
                                    tile(attn_k_gain[li], ATTN_KV_HEADS), ones(ATTN_KV_W), qc_scale])
            q_end = ATTN_Q_W + ATTN_KV_W
            qc_start = q_end + ATTN_KV_W
            proj = _norm_matmul(xf, norm1_gain[layer], w_in_attn, li, post,
                                (0, q_end, qc_start, qc_start + CROSS_W), tm=1024)
            mix_inputs = (_win_attn(proj, attn_sink[li], bsz, seq),)
        else:
            post = jnp.concatenate([ones(RET_QK_W), jnp.full((RET_QK_W,), RET_DK ** -0.5, F32),
                                    ones(2 * RET_V_W), qc_scale])
            qc_start = 2 * RET_QK_W + 2 * RET_V_W
            proj = _norm_matmul(xf, norm1_gain[layer], w_in_ret, li, post,
                                (qc_start, qc_start + CROSS_W, 0, 0), tm=1024)
            fwd, bwd = _retention(proj, ret_decay_fwd[li], ret_decay_bwd[li], bsz, seq)
            gate_col = (2 * RET_QK_W + RET_V_W) // RET_V_W
            mix_inputs = (fwd, bwd, proj, gate_col, ret_out_gain[li])
        cross = _cross_attn(proj, qc_start // CROSS_W, kvm, bsz, seq, mem_len)
        xf, h2 = _out_proj(xf, mix_inputs, cross, w_out, layer, norm2_gain[layer], w_router[layer])
        xf = _moe(xf, h2, w_gate, w_up, w_down, layer, bsz, seq)
    return xf.reshape(bsz, seq, d)
```

```python
import functools
import math

import jax
import jax.numpy as jnp
import numpy as np
from jax import lax
from jax.experimental import pallas as pl
from jax.experimental.pallas import tpu as pltpu

F32 = jnp.float32
BF16 = jnp.bfloat16

HEAD_DIM = 128
ATTN_Q_HEADS = 12
ATTN_KV_HEADS = 4
ATTN_GROUP = ATTN_Q_HEADS // ATTN_KV_HEADS
WINDOW = 128
RET_HEADS = 6
RET_DK = 128
RET_DV = 256
CROSS_HEADS = 4
N_EXPERTS = 16
CAPACITY_FACTOR = 2
NORM_EPS = 1e-6
NEG_INF = -1e30

ATTN_Q_W = ATTN_Q_HEADS * HEAD_DIM
ATTN_KV_W = ATTN_KV_HEADS * HEAD_DIM
CROSS_W = CROSS_HEADS * HEAD_DIM
RET_QK_W = RET_HEADS * RET_DK
RET_V_W = RET_HEADS * RET_DV

LANES = 128
VMEM_LIMIT = 56 * 1024 * 1024
EXPERT_VMEM_LIMIT = 60 * 1024 * 1024

RET_CHUNK = 256
MOE_ROWS = 512
COMBINE_TOKENS = 256
COMBINE_CHUNK = 64
COMBINE_ALIGN = 16


def _cparams(semantics):
    return pltpu.CompilerParams(dimension_semantics=semantics, vmem_limit_bytes=VMEM_LIMIT)


def _alibi_slopes(n):
    def pow2_slopes(m):
        start = 2.0 ** (-(2.0 ** -(math.log2(m) - 3)))
        return [start ** (i + 1) for i in range(m)]
    if math.log2(n).is_integer():
        return pow2_slopes(n)
    closest = 2 ** math.floor(math.log2(n))
    return pow2_slopes(closest) + _alibi_slopes(2 * closest)[0::2][: n - closest]


def _norm_matmul_kernel(x_ref, g_ref, w_ref, ps_ref, o_ref, h_ref, *, norm_tiles):
    j = pl.program_id(1)

    @pl.when(j == 0)
    def _():
        x = x_ref[...]
        ms = jnp.mean(x * x, axis=-1, keepdims=True)
        h_ref[...] = (x * lax.rsqrt(ms + NORM_EPS) * g_ref[...]).astype(BF16)

    acc = jnp.dot(h_ref[...], w_ref[0], preferred_element_type=F32)
    tn = acc.shape[1]
    lo1, hi1, lo2, hi2 = norm_tiles
    normed = ((j >= lo1) & (j < hi1)) | ((j >= lo2) & (j < hi2))

    @pl.when(normed)
    def _():
        for c in range(tn // HEAD_DIM):
            sl = slice(c * HEAD_DIM, (c + 1) * HEAD_DIM)
            a = acc[:, sl]
            ms = jnp.mean(a * a, axis=-1, keepdims=True)
            o_ref[:, sl] = (a * lax.rsqrt(ms + NORM_EPS) * ps_ref[:, sl]).astype(o_ref.dtype)

    @pl.when(jnp.logical_not(normed))
    def _():
        o_ref[...] = (acc * ps_ref[...]).astype(o_ref.dtype)


def _norm_matmul(x, gain, w, layer, post_scale, norm_cols, *, tm, tn=512):
    m, d = x.shape
    n = w.shape[2]
    assert m % tm == 0 and n % tn == 0 and all(c % tn == 0 for c in norm_cols)
    norm_tiles = tuple(c // tn for c in norm_cols)
    return pl.pallas_call(
        functools.partial(_norm_matmul_kernel, norm_tiles=norm_tiles),
        out_shape=jax.ShapeDtypeStruct((m, n), BF16),
        grid=(m // tm, n // tn),
        in_specs=[
            pl.BlockSpec((tm, d), lambda i, j: (i, 0)),
            pl.BlockSpec((1, d), lambda i, j: (0, 0)),
            pl.BlockSpec((1, d, tn), lambda i, j: (layer, 0, j)),
            pl.BlockSpec((1, tn), lambda i, j: (0, j)),
        ],
        out_specs=pl.BlockSpec((tm, tn), lambda i, j: (i, j)),
        scratch_shapes=[pltpu.VMEM((tm, d), BF16)],
        compiler_params=_cparams(("parallel", "arbitrary")),
        name="norm_matmul",
    )(x, gain.reshape(1, d), w, post_scale.reshape(1, n))


def _win_attn_kernel(sink_ref, bias_ref, q_ref, k0_ref, k1_ref, k2_ref, k3_ref, v0_ref, v1_ref, v2_ref, v3_ref,
                     o_ref):
    n = pl.program_id(1)
    nsteps = pl.num_programs(1)
    c = WINDOW
    k_refs = (k0_ref, k1_ref, k2_ref, k3_ref)
    v_refs = (v0_ref, v1_ref, v2_ref, v3_ref)
    col = lax.broadcasted_iota(jnp.int32, (1, 3 * c), 1)
    present = [col >= jnp.where(n > 0, 0, c), col < jnp.where(n < nsteps - 1, 3 * c, 2 * c)]
    nt_dims = (((1,), (1,)), ((), ()))
    chains = [(u, h) for u in range(2) for h in range(ATTN_KV_HEADS)]

    def band(refs, u, h):
        hs = slice(h * HEAD_DIM, (h + 1) * HEAD_DIM)
        return jnp.concatenate([refs[u + j][:, hs] for j in range(3)], axis=0)

    def group_rows(h, make):
        return jnp.concatenate([make(h * ATTN_GROUP + g) for g in range(ATTN_GROUP)], axis=0)

    kb = [band(k_refs, u, h) for u, h in chains]
    ones = jnp.ones((3 * c, HEAD_DIM), BF16)
    vb = [jnp.concatenate([band(v_refs, u, h), ones], axis=1) for u, h in chains]
    q = [group_rows(h, lambda qh: q_ref[u * c:(u + 1) * c, qh * HEAD_DIM:(qh + 1) * HEAD_DIM]) for u, h in chains]
    sink = [group_rows(h, lambda qh: jnp.full((c, 1), sink_ref[qh], F32)) for u, h in chains]
    s = [lax.dot_general(q[i], kb[i], nt_dims, preferred_element_type=F32) for i in range(len(chains))]
    logits = [jnp.where(present[u], s[i] + bias_ref[h], NEG_INF) for i, (u, h) in enumerate(chains)]
    tile_max = [jnp.maximum(jnp.maximum(l[:, :c], l[:, c:2 * c]), l[:, 2 * c:]) for l in logits]
    m = [jnp.maximum(jnp.max(tile_max[i], axis=-1, keepdims=True), sink[i]) for i in range(len(chains))]
    p = [jnp.exp(logits[i] - m[i]).astype(BF16) for i in range(len(chains))]
    pv = [jnp.dot(p[i], vb[i], preferred_element_type=F32) for i in range(len(chains))]
    o = [pv[i][:, :HEAD_DIM] / (pv[i][:, HEAD_DIM:] + jnp.exp(sink[i] - m[i])) for i in range(len(chains))]
    for i, (u, h) in enumerate(chains):
        for g in range(ATTN_GROUP):
            qh = h * ATTN_GROUP + g
            o_ref[u * c:(u + 1) * c, qh * HEAD_DIM:(qh + 1) * HEAD_DIM] = (
                o[i][g * c:(g + 1) * c].astype(o_ref.dtype))


def _alibi_band_bias():
    c = WINDOW
    dist = np.abs(np.arange(c)[:, None] + c - np.arange(3 * c)[None, :])
    slopes = np.array(_alibi_slopes(ATTN_Q_HEADS), np.float32)
    bias = np.where(dist <= WINDOW, -slopes[:, None, None] * dist.astype(np.float32)[None], np.float32(NEG_INF))
    return jnp.asarray(bias.astype(np.float32).reshape(ATTN_KV_HEADS, ATTN_GROUP * c, 3 * c))


def _win_attn(proj, sink, bsz, seq):
    c = WINDOW
    nb = seq // c
    kcol = ATTN_Q_W // ATTN_KV_W
    vcol = kcol + 1

    nsteps = nb // 2

    def kv_spec(col, j):
        def imap(b, n):
            return (b * nb + jnp.clip(2 * n - 1 + j, 0, nb - 1), col)
        return pl.BlockSpec((c, ATTN_KV_W), imap)

    q_spec = pl.BlockSpec((2 * c, ATTN_Q_W), lambda b, n: (b * nsteps + n, 0))
    return pl.pallas_call(
        _win_attn_kernel,
        out_shape=jax.ShapeDtypeStruct((bsz * seq, ATTN_Q_W), BF16),
        grid=(bsz, nsteps),
        in_specs=[
            pl.BlockSpec(memory_space=pltpu.SMEM),
            pl.BlockSpec((ATTN_KV_HEADS, ATTN_GROUP * c, 3 * c), lambda b, n: (0, 0, 0)),
            q_spec,
        ] + [kv_spec(kcol, j) for j in range(4)] + [kv_spec(vcol, j) for j in range(4)],
        out_specs=q_spec,
        compiler_params=_cparams(("parallel", "arbitrary")),
        name="win_attn",
    )(sink, _alibi_band_bias(), *([proj] * 9))


def _cross_attn_kernel(q_ref, kv_ref, o_ref):
    heads = range(CROSS_HEADS)
    mem_len = kv_ref.shape[0]
    nt_dims = (((1,), (1,)), ((), ()))
    hs = [slice(h * HEAD_DIM, (h + 1) * HEAD_DIM) for h in heads]
    ones = jnp.ones((mem_len, HEAD_DIM), BF16)
    vb = [jnp.concatenate([kv_ref[:, CROSS_W + h * HEAD_DIM:CROSS_W + (h + 1) * HEAD_DIM], ones], axis=1)
          for h in heads]
    s = [lax.dot_general(q_ref[:, hs[h]], kv_ref[:, hs[h]], nt_dims, preferred_element_type=F32)
         for h in heads]
    tile_max = [functools.reduce(jnp.maximum, [s[h][:, j:j + LANES] for j in range(0, mem_len, LANES)])
                for h in heads]
    m = [jnp.max(tile_max[h], axis=-1, keepdims=True) for h in heads]
    p = [jnp.exp(s[h] - m[h]).astype(BF16) for h in heads]
    pv = [jnp.dot(p[h], vb[h], preferred_element_type=F32) for h in heads]
    for h in heads:
        o_ref[:, hs[h]] = (pv[h][:, :HEAD_DIM] / pv[h][:, HEAD_DIM:]).astype(o_ref.dtype)


def _cross_attn(proj, qcol, kvm, bsz, seq, mem_len, *, tq=512):
    nq = seq // tq
    return pl.pallas_call(
        _cross_attn_kernel,
        out_shape=jax.ShapeDtypeStruct((bsz * seq, CROSS_W), BF16),
        grid=(bsz, nq),
        in_specs=[
            pl.BlockSpec((tq, CROSS_W), lambda b, i: (b * nq + i, qcol)),
            pl.BlockSpec((mem_len, 2 * CROSS_W), lambda b, i: (b, 0)),
        ],
        out_specs=pl.BlockSpec((tq, CROSS_W), lambda b, i: (b * nq + i, 0)),
        compiler_params=_cparams(("parallel", "arbitrary")),
        name="cross_attn",
    )(proj, kvm)


def _log_sigmoid(x):
    return jnp.minimum(x, 0.0) - jnp.log1p(jnp.exp(-jnp.abs(x)))


def _retention_kernel(df_ref, db_ref, qf_ref, kf_ref, vf_ref, qb_ref, kb_ref, vb_ref,
                      of_ref, ob_ref, state_ref, decay_ref):
    n = pl.program_id(1)
    c = qf_ref.shape[0]
    nt_dims = (((1,), (1,)), ((), ()))
    row = lax.broadcasted_iota(jnp.int32, (c, 1), 0).astype(F32)
    chains = [(h, fwd) for h in range(RET_HEADS) for fwd in (True, False)]
    lg = [_log_sigmoid(jnp.full((1, 1), (df_ref if fwd else db_ref)[h], F32)) for h, fwd in chains]

    @pl.when(n == 0)
    def _():
        state_ref[...] = jnp.zeros_like(state_ref)
        ti = lax.broadcasted_iota(jnp.int32, (c, c), 0)
        si = lax.broadcasted_iota(jnp.int32, (c, c), 1)
        for i, (h, fwd) in enumerate(chains):
            delta = (ti - si) if fwd else (si - ti)
            mask = (delta >= 0) if fwd else (delta > 0)
            decay_ref[i] = jnp.where(mask, jnp.exp(lg[i] * jnp.maximum(delta, 0).astype(F32)), 0.0)

    def qkv(i):
        h, fwd = chains[i]
        q_ref, k_ref, v_ref = (qf_ref, kf_ref, vf_ref) if fwd else (qb_ref, kb_ref, vb_ref)
        return (q_ref[:, h * RET_DK:(h + 1) * RET_DK], k_ref[:, h * RET_DK:(h + 1) * RET_DK],
                v_ref[:, h * RET_DV:(h + 1) * RET_DV])

    ids = range(len(chains))
    scores = [(lax.dot_general(qkv(i)[0], qkv(i)[1], nt_dims, preferred_element_type=F32)
               * decay_ref[i]).astype(BF16) for i in ids]
    xi = [jnp.exp(lg[i] * ((row + 1.0) if chains[i][1] else (c - row))) for i in ids]
    zeta = [jnp.exp(lg[i] * ((c - 1.0 - row) if chains[i][1] else row)) for i in ids]
    qx = [(qkv(i)[0].astype(F32) * xi[i]).astype(BF16) for i in ids]
    kz = [(qkv(i)[1].astype(F32) * zeta[i]).T.astype(BF16) for i in ids]
    state = [state_ref[i] for i in ids]
    out = [jnp.dot(scores[i], qkv(i)[2], preferred_element_type=F32)
           + jnp.dot(qx[i], state[i].astype(BF16), preferred_element_type=F32) for i in ids]
    for i, (h, fwd) in enumerate(chains):
        o_ref = of_ref if fwd else ob_ref
        o_ref[:, h * RET_DV:(h + 1) * RET_DV] = out[i].astype(o_ref.dtype)
        state_ref[i] = jnp.exp(lg[i] * c) * state[i] + jnp.dot(kz[i], qkv(i)[2], preferred_element_type=F32)


def _retention(proj, decay_fwd, decay_bwd, bsz, seq):
    c = RET_CHUNK
    nc = seq // c

    def specs(rev):
        def chunk(b, n):
            return b * nc + ((nc - 1 - n) if rev else n)
        return [
            pl.BlockSpec((c, RET_QK_W), lambda b, n: (chunk(b, n), 0)),
            pl.BlockSpec((c, RET_QK_W), lambda b, n: (chunk(b, n), 1)),
            pl.BlockSpec((c, RET_V_W), lambda b, n: (chunk(b, n), 2 * RET_QK_W // RET_V_W)),
        ], pl.BlockSpec((c, RET_V_W), lambda b, n: (chunk(b, n), 0))

    in_f, out_f = specs(False)
    in_b, out_b = specs(True)
    smem = pl.BlockSpec(memory_space=pltpu.SMEM)
    out = jax.ShapeDtypeStruct((bsz * seq, RET_V_W), BF16)
    n_chains = 2 * RET_HEADS
    return pl.pallas_call(
        _retention_kernel,
        out_shape=(out, out),
        grid=(bsz, nc),
        in_specs=[smem, smem] + in_f + in_b,
        out_specs=(out_f, out_b),
        scratch_shapes=[pltpu.VMEM((n_chains, RET_DK, RET_DV), F32), pltpu.VMEM((n_chains, c, c), F32)],
        compiler_params=_cparams(("parallel", "arbitrary")),
        name="retention",
    )(decay_fwd, decay_bwd, proj, proj, proj, proj, proj, proj)


def _split_bf16(a):
    hi = a.astype(BF16)
    lo = (a - hi.astype(F32)).astype(BF16)
    return hi, lo


def _finish_out_proj(x_ref, mix, cross_ref, w_ref, g2_ref, wr_ref, xo_ref, h2_ref):
    mix_w = mix.shape[1]
    d = x_ref.shape[1]
    y = jnp.dot(mix, w_ref[0, :mix_w, :], preferred_element_type=F32)
    y = y + jnp.dot(cross_ref[...], w_ref[0, mix_w:, :], preferred_element_type=F32)
    x = x_ref[...] + y
    xo_ref[...] = x
    ms = jnp.mean(x * x, axis=-1, keepdims=True)
    h2 = x * lax.rsqrt(ms + NORM_EPS) * g2_ref[...]
    h2_ref[:, 0, :d] = h2
    h_hi, h_lo = _split_bf16(h2)
    w_hi, w_lo = _split_bf16(wr_ref[...])
    both = jnp.dot(h_hi, jnp.concatenate([w_hi, w_lo], axis=1), preferred_element_type=F32)
    logits = both[:, :LANES] + both[:, LANES:] + jnp.dot(h_lo, w_hi, preferred_element_type=F32)
    lane = lax.broadcasted_iota(jnp.int32, logits.shape, 1)
    logits = jnp.where(lane < N_EXPERTS, logits, NEG_INF)
    m = jnp.max(logits, axis=-1, keepdims=True)
    p = jnp.exp(logits - m)
    h2_ref[:, 0, d:] = p / jnp.sum(p, axis=-1, keepdims=True)


def _out_proj_attn_kernel(x_ref, mix_ref, cross_ref, w_ref, g2_ref, wr_ref, xo_ref, h2_ref):
    _finish_out_proj(x_ref, mix_ref[...], cross_ref, w_ref, g2_ref, wr_ref, xo_ref, h2_ref)


def _out_proj_ret_kernel(x_ref, fwd_ref, bwd_ref, gate_ref, og_ref, cross_ref, w_ref, g2_ref, wr_ref,
                         xo_ref, h2_ref):
    parts = []
    for h in range(RET_HEADS):
        hs = slice(h * RET_DV, (h + 1) * RET_DV)
        y = fwd_ref[:, hs].astype(F32) + bwd_ref[:, hs].astype(F32)
        ms = jnp.mean(y * y, axis=-1, keepdims=True)
        y = y * lax.rsqrt(ms + NORM_EPS) * og_ref[:, hs]
        g = gate_ref[:, hs].astype(F32)
        parts.append((g * jax.nn.sigmoid(g) * y).astype(BF16))
    mix = jnp.concatenate(parts, axis=-1)
    _finish_out_proj(x_ref, mix, cross_ref, w_ref, g2_ref, wr_ref, xo_ref, h2_ref)


def _out_proj(x, mix_inputs, cross, w_out, layer, g2, w_router, *, tm=512):
    m, d = x.shape
    row = lambda i: (i, 0)
    const = lambda i: (0, 0)
    tail_specs = [
        pl.BlockSpec((tm, CROSS_W), row),
        pl.BlockSpec((1, d, d), lambda i: (layer, 0, 0)),
        pl.BlockSpec((1, d), const),
        pl.BlockSpec((d, LANES), const),
    ]
    w_router = jnp.pad(w_router, ((0, 0), (0, LANES - N_EXPERTS)))
    tail_args = (cross, w_out, g2.reshape(1, d), w_router)
    if len(mix_inputs) == 1:
        kernel = _out_proj_attn_kernel
        specs = [pl.BlockSpec((tm, d), row), pl.BlockSpec((tm, ATTN_Q_W), row)] + tail_specs
        args = (x, mix_inputs[0]) + tail_args
    else:
        fwd, bwd, proj, gate_col, out_gain = mix_inputs
        kernel = _out_proj_ret_kernel
        specs = [
            pl.BlockSpec((tm, d), row),
            pl.BlockSpec((tm, RET_V_W), row),
            pl.BlockSpec((tm, RET_V_W), row),
            pl.BlockSpec((tm, RET_V_W), lambda i: (i, gate_col)),
            pl.BlockSpec((1, RET_V_W), const),
        ] + tail_specs
        args = (x, fwd, bwd, proj, out_gain.reshape(1, RET_V_W)) + tail_args
    return pl.pallas_call(
        kernel,
        out_shape=(jax.ShapeDtypeStruct((m, d), F32), jax.ShapeDtypeStruct((m, 1, d + LANES), F32)),
        grid=(m // tm,),
        in_specs=specs,
        out_specs=(pl.BlockSpec((tm, d), row), pl.BlockSpec((tm, 1, d + LANES), lambda i: (i, 0, 0))),
        compiler_params=_cparams(("parallel",)),
        name="out_proj",
    )(*args)


def _route_kernel(aff_ref, idx_ref, pos_ref, cum_ref, *, cap):
    aff = aff_ref[0]
    e, nt, _ = aff.shape
    rows = e * nt
    bits = pltpu.bitcast(aff, jnp.int32)
    capf = float(cap)

    def count(mask):
        c = jnp.sum(jnp.where(mask, 1.0, 0.0), axis=1, keepdims=True)
        return jnp.sum(c, axis=2, keepdims=True)

    def search(_, lohi):
        lo, hi = lohi
        mid = lo + ((hi - lo + 1) >> 1)
        ok = count(bits >= mid) >= capf
        return jnp.where(ok, mid, lo), jnp.where(ok, hi, mid - 1)

    lo0 = jnp.zeros((e, 1, 1), jnp.int32)
    hi0 = jnp.full((e, 1, 1), 0x7F800000, jnp.int32)
    thr, _ = lax.fori_loop(0, 31, search, (lo0, hi0))

    ri = lax.broadcasted_iota(jnp.int32, (LANES, LANES), 0)
    ci = lax.broadcasted_iota(jnp.int32, (LANES, LANES), 1)
    tri = jnp.where(ri <= ci, 1.0, 0.0).astype(BF16)
    shift = nt.bit_length() - 1
    rr = lax.broadcasted_iota(jnp.int32, (rows, rows), 0)
    cc = lax.broadcasted_iota(jnp.int32, (rows, rows), 1)
    earlier_tile = jnp.where(((rr >> shift) == (cc >> shift)) & (cc < rr), 1.0, 0.0).astype(BF16)

    def prefix(m3):
        m2 = m3.reshape(rows, LANES).astype(BF16)
        inc = jnp.dot(m2, tri, preferred_element_type=F32)
        tot = jnp.broadcast_to(inc[:, LANES - 1:LANES], (rows, LANES)).astype(BF16)
        offs = jnp.dot(earlier_tile, tot, preferred_element_type=F32)
        return (inc + offs).reshape(e, nt, LANES)

    gt = bits > thr
    eq = bits == thr
    need = capf - count(gt)
    eqf = jnp.where(eq, 1.0, 0.0)
    tie_rank = prefix(eqf) - eqf
    sel = gt | (eq & (tie_rank < need))
    self_ = jnp.where(sel, 1.0, 0.0)
    cum = prefix(self_)
    pos = (cum - self_).astype(jnp.int32)
    pos_ref[0] = jnp.where(sel, pos, -1 - pos)
    cum_ref[...] = cum

    nt_dims = (((1,), (1,)), ((), ()))
    kcol = lax.broadcasted_iota(jnp.int32, (cap, 1), 0).astype(F32)
    lane8 = lax.broadcasted_iota(jnp.int32, (8, LANES), 1)
    pick_last = jnp.where(lane8 == LANES - 1, 1.0, 0.0).astype(BF16)
    ones_nt = jnp.ones((8, nt), BF16)
    ones_ln = jnp.ones((8, LANES), BF16)

    def expert_list(ex, carry):
        cum_e = cum_ref[ex]
        hi = jnp.floor(cum_e * (1.0 / 32.0))
        lo = cum_e - 32.0 * hi
        hib, lob = hi.astype(BF16), lo.astype(BF16)
        tile_end = (32.0 * lax.dot_general(pick_last, hib, nt_dims, preferred_element_type=F32)
                    + lax.dot_general(pick_last, lob, nt_dims, preferred_element_type=F32))
        done = jnp.where(tile_end[0:1, :] <= kcol, 1.0, 0.0).astype(BF16)
        full = lax.dot_general(ones_nt, done, nt_dims, preferred_element_type=F32)
        step_hi = (pltpu.roll(hi, nt - 1, 0) - hi).astype(BF16)
        step_lo = (pltpu.roll(lo, nt - 1, 0) - lo).astype(BF16)
        sel_hi = jnp.dot(done, step_hi, preferred_element_type=F32) + hi[0:1, :]
        sel_lo = jnp.dot(done, step_lo, preferred_element_type=F32) + lo[0:1, :]
        inside = jnp.where(32.0 * sel_hi + sel_lo <= kcol, 1.0, 0.0).astype(BF16)
        part = lax.dot_general(ones_ln, inside, nt_dims, preferred_element_type=F32)
        idx_ref[0, pl.ds(pl.multiple_of(ex * 8, 8), 8), :] = (LANES * full + part).astype(jnp.int32)
        return carry

    lax.fori_loop(0, e, expert_list, 0)


def _route(aff_t, cap):
    bsz, e, s = aff_t.shape
    nt = s // LANES
    assert nt & (nt - 1) == 0
    blk = pl.BlockSpec((1, e, nt, LANES), lambda b: (b, 0, 0, 0))
    idx8, pos = pl.pallas_call(
        functools.partial(_route_kernel, cap=cap),
        out_shape=(jax.ShapeDtypeStruct((bsz, e * 8, cap), jnp.int32),
                   jax.ShapeDtypeStruct((bsz, e, nt, LANES), jnp.int32)),
        grid=(bsz,),
        in_specs=[blk],
        out_specs=(pl.BlockSpec((1, e * 8, cap), lambda b: (b, 0, 0)), blk),
        scratch_shapes=[pltpu.VMEM((e, nt, LANES), F32)],
        compiler_params=_cparams(("parallel",)),
        name="route",
    )(aff_t.reshape(bsz, e, nt, LANES))
    return idx8[:, ::8, :], pos.reshape(bsz, e, s)


def _expert_kernel(idx_ref, h_hbm, wg_hbm, wu_hbm, wd_hbm, y_ref, xa, xb, sem, wg_s, wu_s, wd_s, stg_a, stg_d,
                   wsem, *, layer, cap, d):
    e = pl.program_id(0)
    b = pl.program_id(1)
    r = pl.program_id(2)
    n_e = pl.num_programs(0)
    n_b = pl.num_programs(1)
    rows = xa.shape[0]
    step = (e * n_b + b) * 2 + r
    n_steps = n_e * n_b * 2
    sub = b * 2 + r
    n_sub = n_b * 2
    ca = stg_a.shape[1]
    cd = stg_d.shape[1]

    def pieces(s):
        out = []
        for j in range(2):
            ra = pl.multiple_of((2 * s + j) * ca, ca)
            rd = pl.multiple_of((2 * s + j) * cd, cd)
            out.append((wg_hbm, wg_s, ra, ca, stg_a.at[j]))
            out.append((wu_hbm, wu_s, ra, ca, stg_a.at[2 + j]))
            out.append((wd_hbm, wd_s, rd, cd, stg_d.at[j]))
        return out

    def stage_copies(ee, s):
        return [pltpu.make_async_copy(src.at[layer, ee, pl.ds(r0, n), :], stg, wsem.at[0])
                for src, _, r0, n, stg in pieces(s)]

    def convert(dst_set, s):
        for _, dst, r0, n, stg in pieces(s):
            dst[dst_set, pl.ds(r0, n), :] = stg[...].astype(BF16)

    next_expert = lambda ee: jnp.where(ee == n_e - 1, 0, ee + 1)
    cur_set = e % 2

    def gather(bb, ee, rr, buf, slot, unrolled):
        base = (bb * n_e + ee) * cap + rr * rows

        def start(i):
            pltpu.make_async_copy(h_hbm.at[idx_ref[base + i]], buf.at[pl.ds(i, 1), :], sem.at[slot]).start()

        if unrolled:
            for i in range(rows):
                start(i)
        else:
            def body(i, carry):
                start(i)
                return carry
            lax.fori_loop(0, rows, body, 0)

    def wait(buf, slot):
        pltpu.make_async_copy(h_hbm.at[pl.ds(0, rows), 0, :], buf, sem.at[slot]).wait()

    wrap_b = r == 1
    wrap_e = wrap_b & (b == n_b - 1)
    b1 = jnp.where(wrap_b, jnp.where(b == n_b - 1, 0, b + 1), b)
    e1 = jnp.where(wrap_e, jnp.where(e == n_e - 1, 0, e + 1), e)

    @pl.when(step == 0)
    def _():
        gather(b, e, r, xa, 0, False)
        for s in range(n_sub):
            for cp in stage_copies(0, s):
                cp.start()
            for cp in stage_copies(0, s):
                cp.wait()
            convert(0, s)
        for cp in stage_copies(next_expert(e), sub):
            cp.start()

    def compute(cur, cur_slot, nxt, nxt_slot):
        gather(b1, e1, 1 - r, nxt, nxt_slot, True)
        wait(cur, cur_slot)
        x = cur[:, :d].astype(BF16)
        hg = jnp.dot(x, wg_s[cur_set], preferred_element_type=F32)
        for cp in stage_copies(next_expert(e), sub):
            cp.wait()
        convert(1 - cur_set, sub)
        hu = jnp.dot(x, wu_s[cur_set], preferred_element_type=F32)
        hid = (hg * jax.nn.sigmoid(hg) * hu).astype(BF16)
        lane = lax.broadcasted_iota(jnp.int32, (rows, LANES), 1)
        gate = jnp.sum(jnp.where(lane == e, cur[:, d:], 0.0), axis=-1, keepdims=True)
        half = d // 2
        for c in range(2):
            cols = slice(c * half, (c + 1) * half)
            y = jnp.dot(hid, wd_s[cur_set, :, cols], preferred_element_type=F32)
            y_ref[0, 0, :, cols] = (y * gate).astype(y_ref.dtype)

        @pl.when(step < n_steps - 1)
        def _():
            last_sub = sub == n_sub - 1
            e_next = jnp.where(last_sub, e + 1, e)
            for cp in stage_copies(next_expert(e_next), jnp.where(last_sub, 0, sub + 1)):
                cp.start()

    @pl.when(r == 0)
    def _():
        compute(xa, 0, xb, 1)

    @pl.when(r == 1)
    def _():
        compute(xb, 1, xa, 0)

    @pl.when(step == n_steps - 1)
    def _():
        wait(xa, 0)


def _experts(idx, h2, wg, wu, wd, layer):
    bsz, e, cap = idx.shape
    d = h2.shape[2] - LANES
    ff = wg.shape[3]
    rows = cap // 2
    n_pieces = 2 * 2 * bsz
    any_space = pl.BlockSpec(memory_space=pl.ANY)
    grid_spec = pltpu.PrefetchScalarGridSpec(
        num_scalar_prefetch=1,
        grid=(e, bsz, 2),
        in_specs=[any_space, any_space, any_space, any_space],
        out_specs=pl.BlockSpec((1, 1, rows, d), lambda ei, b, r, idx: (b, ei, r, 0)),
        scratch_shapes=[pltpu.VMEM((rows, d + LANES), F32), pltpu.VMEM((rows, d + LANES), F32),
                        pltpu.SemaphoreType.DMA((2,)),
                        pltpu.VMEM((2, d, ff), BF16), pltpu.VMEM((2, d, ff), BF16), pltpu.VMEM((2, ff, d), BF16),
                        pltpu.VMEM((4, d // n_pieces, ff), F32), pltpu.VMEM((2, ff // n_pieces, d), F32),
                        pltpu.SemaphoreType.DMA((1,))],
    )
    return pl.pallas_call(
        functools.partial(_expert_kernel, layer=layer, cap=cap, d=d),
        out_shape=jax.ShapeDtypeStruct((bsz, e, cap, d), BF16),
        grid_spec=grid_spec,
        compiler_params=pltpu.CompilerParams(dimension_semantics=("arbitrary", "arbitrary", "arbitrary"),
                                             vmem_limit_bytes=EXPERT_VMEM_LIMIT),
        name="experts",
    )(idx.reshape(-1), h2, wg, wu, wd)


def _combine_kernel(off_ref, x_ref, pos_ref, y_hbm, o_ref, ybuf, sem, *, cap, n_tiles):
    b = pl.program_id(0)
    i = pl.program_id(1)
    n_b = pl.num_programs(0)
    tt = x_ref.shape[0]
    n_exp = pos_ref.shape[1]
    ch = COMBINE_CHUNK
    per_tile = LANES // ch
    step = b * n_tiles + i
    slot = step % 2
    last_start = cap - ch

    def tile_rows(bb, ii):
        base = (bb * (n_tiles + 1) + ii) * n_exp
        first = [(off_ref[base + ex] // COMBINE_ALIGN) * COMBINE_ALIGN for ex in range(n_exp)]
        end = [off_ref[base + n_exp + ex] for ex in range(n_exp)]
        return first, end

    def chunk_start(lo):
        return pl.multiple_of(jnp.minimum(lo, last_start), COMBINE_ALIGN)

    def chunk_copy(bb, ex, lo, sl):
        return pltpu.make_async_copy(y_hbm.at[bb, ex, pl.ds(chunk_start(lo), ch), :],
                                     ybuf.at[sl, pl.ds(ex * ch, ch), :], sem.at[sl])

    def fetch(bb, los, sl):
        for ex in range(n_exp):
            chunk_copy(bb, ex, los[ex], sl).start()

    def wait(sl):
        for ex in range(n_exp):
            chunk_copy(0, ex, 0, sl).wait()

    lane = lax.broadcasted_iota(jnp.int32, (tt, LANES), 1)

    def onehot(los):
        blocks = []
        for j in range(n_exp // per_tile):
            target = jnp.full((tt, LANES), -1, jnp.int32)
            for u in range(per_tile):
                ex = j * per_tile + u
                p = pos_ref[:, ex:ex + 1]
                lo = los[ex]
                q = jnp.where((p >= lo) & (p < lo + ch), p - chunk_start(lo) + u * ch, -1)
                target = jnp.where((lane >= u * ch) & (lane < (u + 1) * ch), q, target)
            blocks.append(jnp.where(target == lane, 1.0, 0.0).astype(BF16))
        return jnp.concatenate(blocks, axis=1)

    first, end = tile_rows(b, i)

    @pl.when(step == 0)
    def _():
        fetch(b, first, slot)

    @pl.when(step < n_b * n_tiles - 1)
    def _():
        wrap = i == n_tiles - 1
        nxt, _ = tile_rows(jnp.where(wrap, b + 1, b), jnp.where(wrap, 0, i + 1))
        fetch(jnp.where(wrap, b + 1, b), nxt, 1 - slot)

    wait(slot)
    o_ref[...] = x_ref[...] + jnp.dot(onehot(first), ybuf[slot], preferred_element_type=F32)

    n_pass = functools.reduce(jnp.maximum, [(end[ex] - first[ex] + ch - 1) // ch for ex in range(n_exp)])

    def extra(c, carry):
        los = [first[ex] + c * ch for ex in range(n_exp)]
        fetch(b, los, slot)
        wait(slot)
        o_ref[...] += jnp.dot(onehot(los), ybuf[slot], preferred_element_type=F32)
        return carry

    lax.fori_loop(1, n_pass, extra, 0)


def _combine(x, pos, offsets, y, bsz, seq):
    d = x.shape[1]
    e, cap = y.shape[1], y.shape[2]
    tt = COMBINE_TOKENS
    n_tiles = seq // tt
    row = lambda b, i, off: (b * n_tiles + i, 0)
    grid_spec = pltpu.PrefetchScalarGridSpec(
        num_scalar_prefetch=1,
        grid=(bsz, n_tiles),
        in_specs=[
            pl.BlockSpec((tt, d), row),
            pl.BlockSpec((tt, e), row),
            pl.BlockSpec(memory_space=pl.ANY),
        ],
        out_specs=pl.BlockSpec((tt, d), row),
        scratch_shapes=[pltpu.VMEM((2, e * COMBINE_CHUNK, d), BF16), pltpu.SemaphoreType.DMA((2,))],
    )
    return pl.pallas_call(
        functools.partial(_combine_kernel, cap=cap, n_tiles=n_tiles),
        out_shape=jax.ShapeDtypeStruct(x.shape, F32),
        grid_spec=grid_spec,
        compiler_params=_cparams(("arbitrary", "arbitrary")),
        name="combine",
    )(offsets.reshape(-1), x, pos, y)


def _moe(x, h2, wg, wu, wd, layer, bsz, seq):
    d = x.shape[1]
    cap = CAPACITY_FACTOR * seq // N_EXPERTS
    aff_t = jnp.swapaxes(h2[:, 0, d:d + N_EXPERTS].reshape(bsz, seq, N_EXPERTS), 1, 2)
    idx, pos_t = _route(aff_t, cap)
    rows_of = idx + (jnp.arange(bsz, dtype=jnp.int32) * seq)[:, None, None]
    y = _experts(rows_of, h2, wg, wu, wd, layer)
    first = pos_t[:, :, ::COMBINE_TOKENS]
    offsets = jnp.concatenate(
        [jnp.where(first >= 0, first, -1 - first), jnp.full((bsz, N_EXPERTS, 1), cap, jnp.int32)], axis=-1)
    offsets = jnp.swapaxes(offsets, 1, 2)
    pos = jnp.swapaxes(pos_t, 1, 2).reshape(bsz * seq, N_EXPERTS)
    return _combine(x, pos, offsets, y, bsz, seq)


def kernel(x, mem, norm1_gain, w_in_attn, attn_q_gain, attn_k_gain, attn_sink, w_in_ret, ret_decay_fwd,
           ret_decay_bwd, ret_out_gain, mem_norm_gain, w_mem_kv, cross_q_gain, cross_k_gain, w_out,
           norm2_gain, w_router, w_gate, w_up, w_down):
    bsz, seq, d = x.shape
    mem_len = mem.shape[1]
    depth = norm1_gain.shape[0]
    xf = x.reshape(bsz * seq, d)
    memf = mem.reshape(bsz * mem_len, d)
    scale = HEAD_DIM ** -0.5
    ones = lambda n: jnp.ones((n,), F32)
    tile = lambda g, n: jnp.tile(g, n)
    w_in_attn, w_in_ret, w_mem_kv, w_out = (w.astype(BF16) for w in (w_in_attn, w_in_ret, w_mem_kv, w_out))
    for layer in range(depth):
        li = layer // 2
        kv_scale = jnp.concatenate([tile(cross_k_gain[layer], CROSS_HEADS), ones(CROSS_W)])
        kvm = _norm_matmul(memf, mem_norm_gain[layer], w_mem_kv, layer, kv_scale,
                           (0, CROSS_W, 0, 0), tm=bsz * mem_len)
        qc_scale = tile(cross_q_gain[layer] * scale, CROSS_HEADS)
        if layer % 2 == 0:
            post = jnp.concatenate([tile(attn_q_gain[li] * scale, ATTN_Q_HEADS),
                                    tile(attn_k_gain[li], ATTN_KV_HEADS), ones(ATTN_KV_W), qc_scale])
            q_end = ATTN_Q_W + ATTN_KV_W
            qc_start = q_end + ATTN_KV_W
            proj = _norm_matmul(xf, norm1_gain[layer], w_in_attn, li, post,
                                (0, q_end, qc_start, qc_start + CROSS_W), tm=1024)
            mix_inputs = (_win_attn(proj, attn_sink[li], bsz, seq),)
        else:
            post = jnp.concatenate([ones(RET_QK_W), jnp.full((RET_QK_W,), RET_DK ** -0.5, F32),
                                    ones(2 * RET_V_W), qc_scale])
            qc_start = 2 * RET_QK_W + 2 * RET_V_W
            proj = _norm_matmul(xf, norm1_gain[layer], w_in_ret, li, post,
                                (qc_start, qc_start + CROSS_W, 0, 0), tm=1024)
            fwd, bwd = _retention(proj, ret_decay_fwd[li], ret_decay_bwd[li], bsz, seq)
            gate_col = (2 * RET_QK_W + RET_V_W) // RET_V_W
            mix_inputs = (fwd, bwd, proj, gate_col, ret_out_gain[li])
        cross = _cross_attn(proj, qc_start // CROSS_W, kvm, bsz, seq, mem_len)
        xf, h2 = _out_proj(xf, mix_inputs, cross, w_out, layer, norm2_gain[layer], w_router[layer])
        xf = _moe(xf, h2, w_gate, w_up, w_down, layer, bsz, seq)
    return xf.reshape(bsz, seq, d)
```

```python
import functools
import math

import jax
import jax.numpy as jnp
import numpy as np
from jax import lax
from jax.experimental import pallas as pl
from jax.experimental.pallas import tpu as pltpu

F32 = jnp.float32
BF16 = jnp.bfloat16

HEAD_DIM = 128
ATTN_Q_HEADS = 12
ATTN_KV_HEADS = 4
ATTN_GROUP = ATTN_Q_HEADS // ATTN_KV_HEADS
WINDOW = 128
RET_HEADS = 6
RET_DK = 128
RET_DV = 256
CROSS_HEADS = 4
N_EXPERTS = 16
CAPACITY_FACTOR = 2
NORM_EPS = 1e-6
NEG_INF = -1e30

ATTN_Q_W = ATTN_Q_HEADS * HEAD_DIM
ATTN_KV_W = ATTN_KV_HEADS * HEAD_DIM
CROSS_W = CROSS_HEADS * HEAD_DIM
RET_QK_W = RET_HEADS * RET_DK
RET_V_W = RET_HEADS * RET_DV

LANES = 128
VMEM_LIMIT = 56 * 1024 * 1024
EXPERT_VMEM_LIMIT = 60 * 1024 * 1024

NORM_PIECES = 4
RET_CHUNK = 256
MOE_ROWS = 512
COMBINE_TOKENS = 256
COMBINE_CHUNK = 64
COMBINE_ALIGN = 16


def _cparams(semantics):
    return pltpu.CompilerParams(dimension_semantics=semantics, vmem_limit_bytes=VMEM_LIMIT)


def _alibi_slopes(n):
    def pow2_slopes(m):
        start = 2.0 ** (-(2.0 ** -(math.log2(m) - 3)))
        return [start ** (i + 1) for i in range(m)]
    if math.log2(n).is_integer():
        return pow2_slopes(n)
    closest = 2 ** math.floor(math.log2(n))
    return pow2_slopes(closest) + _alibi_slopes(2 * closest)[0::2][: n - closest]


def _norm_matmul_kernel(x_ref, xn_ref, g_ref, w_ref, ps_ref, o_ref, ha_ref, hb_ref, *, norm_tiles):
    i = pl.program_id(0)
    j = pl.program_id(1)
    tm = x_ref.shape[0]
    piece = tm // NORM_PIECES
    lo1, hi1, lo2, hi2 = norm_tiles
    normed = ((j >= lo1) & (j < hi1)) | ((j >= lo2) & (j < hi2))

    def normed_rows(ref, r0, n):
        x = ref[pl.ds(r0, n), :]
        ms = jnp.mean(x * x, axis=-1, keepdims=True)
        return (x * lax.rsqrt(ms + NORM_EPS) * g_ref[...]).astype(BF16)

    @pl.when((i == 0) & (j == 0))
    def _():
        ha_ref[...] = normed_rows(x_ref, 0, tm)

    def body(h_cur, h_nxt):
        r0 = pl.multiple_of(jnp.minimum(j, NORM_PIECES - 1) * piece, piece)
        h_nxt[pl.ds(r0, piece), :] = normed_rows(xn_ref, r0, piece)
        acc = jnp.dot(h_cur[...], w_ref[0], preferred_element_type=F32)
        tn = acc.shape[1]

        @pl.when(normed)
        def _():
            for c in range(tn // HEAD_DIM):
                sl = slice(c * HEAD_DIM, (c + 1) * HEAD_DIM)
                a = acc[:, sl]
                ms = jnp.mean(a * a, axis=-1, keepdims=True)
                o_ref[:, sl] = (a * lax.rsqrt(ms + NORM_EPS) * ps_ref[:, sl]).astype(o_ref.dtype)

        @pl.when(jnp.logical_not(normed))
        def _():
            o_ref[...] = (acc * ps_ref[...]).astype(o_ref.dtype)

    @pl.when(i % 2 == 0)
    def _():
        body(ha_ref, hb_ref)

    @pl.when(i % 2 == 1)
    def _():
        body(hb_ref, ha_ref)


def _norm_matmul(x, gain, w, layer, post_scale, norm_cols, *, tm, tn=512):
    m, d = x.shape
    n = w.shape[2]
    n_rows, n_cols = m // tm, n // tn
    assert m % tm == 0 and n % tn == 0 and all(c % tn == 0 for c in norm_cols)
    assert n_rows == 1 or n_cols >= NORM_PIECES
    norm_tiles = tuple(c // tn for c in norm_cols)
    return pl.pallas_call(
        functools.partial(_norm_matmul_kernel, norm_tiles=norm_tiles),
        out_shape=jax.ShapeDtypeStruct((m, n), BF16),
        grid=(n_rows, n_cols),
        in_specs=[
            pl.BlockSpec((tm, d), lambda i, j: (i, 0)),
            pl.BlockSpec((tm, d), lambda i, j: (jnp.minimum(i + 1, n_rows - 1), 0)),
            pl.BlockSpec((1, d), lambda i, j: (0, 0)),
            pl.BlockSpec((1, d, tn), lambda i, j: (layer, 0, j)),
            pl.BlockSpec((1, tn), lambda i, j: (0, j)),
        ],
        out_specs=pl.BlockSpec((tm, tn), lambda i, j: (i, j)),
        scratch_shapes=[pltpu.VMEM((tm, d), BF16), pltpu.VMEM((tm, d), BF16)],
        compiler_params=_cparams(("arbitrary", "arbitrary")),
        name="norm_matmul",
    )(x, x, gain.reshape(1, d), w, post_scale.reshape(1, n))


def _win_attn_kernel(sink_ref, bias_ref, q_ref, k0_ref, k1_ref, k2_ref, k3_ref, v0_ref, v1_ref, v2_ref, v3_ref,
                     o_ref):
    n = pl.program_id(1)
    nsteps = pl.num_programs(1)
    c = WINDOW
    k_refs = (k0_ref, k1_ref, k2_ref, k3_ref)
    v_refs = (v0_ref, v1_ref, v2_ref, v3_ref)
    col = lax.broadcasted_iota(jnp.int32, (1, 3 * c), 1)
    present = [col >= jnp.where(n > 0, 0, c), col < jnp.where(n < nsteps - 1, 3 * c, 2 * c)]
    nt_dims = (((1,), (1,)), ((), ()))
    chains = [(u, h) for u in range(2) for h in range(ATTN_KV_HEADS)]

    def band(refs, u, h):
        hs = slice(h * HEAD_DIM, (h + 1) * HEAD_DIM)
        return jnp.concatenate([refs[u + j][:, hs] for j in range(3)], axis=0)

    def group_rows(h, make):
        return jnp.concatenate([make(h * ATTN_GROUP + g) for g in range(ATTN_GROUP)], axis=0)

    kb = [band(k_refs, u, h) for u, h in chains]
    ones = jnp.ones((3 * c, HEAD_DIM), BF16)
    vb = [jnp.concatenate([band(v_refs, u, h), ones], axis=1) for u, h in chains]
    q = [group_rows(h, lambda qh: q_ref[u * c:(u + 1) * c, qh * HEAD_DIM:(qh + 1) * HEAD_DIM]) for u, h in chains]
    sink = [group_rows(h, lambda qh: jnp.full((c, HEAD_DIM), sink_ref[qh], F32)) for u, h in chains]
    s = [lax.dot_general(q[i], kb[i], nt_dims, preferred_element_type=F32) for i in range(len(chains))]
    logits = [jnp.where(present[u], s[i] + bias_ref[h], NEG_INF) for i, (u, h) in enumerate(chains)]
    tile_max = [jnp.maximum(jnp.maximum(l[:, :c], l[:, c:2 * c]), l[:, 2 * c:]) for l in logits]
    m = [jnp.maximum(jnp.broadcast_to(jnp.max(tile_max[i], axis=-1, keepdims=True), sink[i].shape), sink[i])
         for i in range(len(chains))]
    p = [jnp.concatenate([jnp.exp(logits[i][:, j * c:(j + 1) * c] - m[i]) for j in range(3)], axis=1).astype(BF16)
         for i in range(len(chains))]
    pv = [jnp.dot(p[i], vb[i], preferred_element_type=F32) for i in range(len(chains))]
    o = [pv[i][:, :HEAD_DIM] / (pv[i][:, HEAD_DIM:] + jnp.exp(sink[i] - m[i])) for i in range(len(chains))]
    for i, (u, h) in enumerate(chains):
        for g in range(ATTN_GROUP):
            qh = h * ATTN_GROUP + g
            o_ref[u * c:(u + 1) * c, qh * HEAD_DIM:(qh + 1) * HEAD_DIM] = (
                o[i][g * c:(g + 1) * c].astype(o_ref.dtype))


def _alibi_band_bias():
    c = WINDOW
    dist = np.abs(np.arange(c)[:, None] + c - np.arange(3 * c)[None, :])
    slopes = np.array(_alibi_slopes(ATTN_Q_HEADS), np.float32)
    bias = np.where(dist <= WINDOW, -slopes[:, None, None] * dist.astype(np.float32)[None], np.float32(NEG_INF))
    return jnp.asarray(bias.astype(np.float32).reshape(ATTN_KV_HEADS, ATTN_GROUP * c, 3 * c))


def _win_attn(proj, sink, bsz, seq):
    c = WINDOW
    nb = seq // c
    kcol = ATTN_Q_W // ATTN_KV_W
    vcol = kcol + 1

    nsteps = nb // 2

    def kv_spec(col, j):
        def imap(b, n):
            return (b * nb + jnp.clip(2 * n - 1 + j, 0, nb - 1), col)
        return pl.BlockSpec((c, ATTN_KV_W), imap)

    q_spec = pl.BlockSpec((2 * c, ATTN_Q_W), lambda b, n: (b * nsteps + n, 0))
    return pl.pallas_call(
        _win_attn_kernel,
        out_shape=jax.ShapeDtypeStruct((bsz * seq, ATTN_Q_W), BF16),
        grid=(bsz, nsteps),
        in_specs=[
            pl.BlockSpec(memory_space=pltpu.SMEM),
            pl.BlockSpec((ATTN_KV_HEADS, ATTN_GROUP * c, 3 * c), lambda b, n: (0, 0, 0)),
            q_spec,
        ] + [kv_spec(kcol, j) for j in range(4)] + [kv_spec(vcol, j) for j in range(4)],
        out_specs=q_spec,
        compiler_params=_cparams(("parallel", "arbitrary")),
        name="win_attn",
    )(sink, _alibi_band_bias(), *([proj] * 9))


def _cross_attn_kernel(q_ref, kv_ref, o_ref):
    heads = range(CROSS_HEADS)
    mem_len = kv_ref.shape[0]
    nt_dims = (((1,), (1,)), ((), ()))
    hs = [slice(h * HEAD_DIM, (h + 1) * HEAD_DIM) for h in heads]
    ones = jnp.ones((mem_len, HEAD_DIM), BF16)
    vb = [jnp.concatenate([kv_ref[:, CROSS_W + h * HEAD_DIM:CROSS_W + (h + 1) * HEAD_DIM], ones], axis=1)
          for h in heads]
    s = [lax.dot_general(q_ref[:, hs[h]], kv_ref[:, hs[h]], nt_dims, preferred_element_type=F32)
         for h in heads]
    tile_max = [functools.reduce(jnp.maximum, [s[h][:, j:j + LANES] for j in range(0, mem_len, LANES)])
                for h in heads]
    m = [jnp.broadcast_to(jnp.max(tile_max[h], axis=-1, keepdims=True), tile_max[h].shape) for h in heads]
    p = [jnp.concatenate([jnp.exp(s[h][:, j:j + LANES] - m[h]) for j in range(0, mem_len, LANES)],
                         axis=1).astype(BF16) for h in heads]
    pv = [jnp.dot(p[h], vb[h], preferred_element_type=F32) for h in heads]
    for h in heads:
        o_ref[:, hs[h]] = (pv[h][:, :HEAD_DIM] / pv[h][:, HEAD_DIM:]).astype(o_ref.dtype)


def _cross_attn(proj, qcol, kvm, bsz, seq, mem_len, *, tq=512):
    nq = seq // tq
    return pl.pallas_call(
        _cross_attn_kernel,
        out_shape=jax.ShapeDtypeStruct((bsz * seq, CROSS_W), BF16),
        grid=(bsz, nq),
        in_specs=[
            pl.BlockSpec((tq, CROSS_W), lambda b, i: (b * nq + i, qcol)),
            pl.BlockSpec((mem_len, 2 * CROSS_W), lambda b, i: (b, 0)),
        ],
        out_specs=pl.BlockSpec((tq, CROSS_W), lambda b, i: (b * nq + i, 0)),
        compiler_params=_cparams(("parallel", "arbitrary")),
        name="cross_attn",
    )(proj, kvm)


def _log_sigmoid(x):
    return jnp.minimum(x, 0.0) - jnp.log1p(jnp.exp(-jnp.abs(x)))


def _retention_kernel(df_ref, db_ref, qf_ref, kf_ref, vf_ref, qb_ref, kb_ref, vb_ref,
                      of_ref, ob_ref, state_ref, decay_ref):
    n = pl.program_id(1)
    c = qf_ref.shape[0]
    nt_dims = (((1,), (1,)), ((), ()))
    row = lax.broadcasted_iota(jnp.int32, (c, 1), 0).astype(F32)
    chains = [(h, fwd) for h in range(RET_HEADS) for fwd in (True, False)]
    lg = [_log_sigmoid(jnp.full((1, 1), (df_ref if fwd else db_ref)[h], F32)) for h, fwd in chains]

    @pl.when(n == 0)
    def _():
        state_ref[...] = jnp.zeros_like(state_ref)
        ti = lax.broadcasted_iota(jnp.int32, (c, c), 0)
        si = lax.broadcasted_iota(jnp.int32, (c, c), 1)
        for i, (h, fwd) in enumerate(chains):
            delta = (ti - si) if fwd else (si - ti)
            mask = (delta >= 0) if fwd else (delta > 0)
            decay_ref[i] = jnp.where(mask, jnp.exp(lg[i] * jnp.maximum(delta, 0).astype(F32)), 0.0)

    def qkv(i):
        h, fwd = chains[i]
        q_ref, k_ref, v_ref = (qf_ref, kf_ref, vf_ref) if fwd else (qb_ref, kb_ref, vb_ref)
        return (q_ref[:, h * RET_DK:(h + 1) * RET_DK], k_ref[:, h * RET_DK:(h + 1) * RET_DK],
                v_ref[:, h * RET_DV:(h + 1) * RET_DV])

    ids = range(len(chains))
    scores = [(lax.dot_general(qkv(i)[0], qkv(i)[1], nt_dims, preferred_element_type=F32)
               * decay_ref[i]).astype(BF16) for i in ids]
    xi = [jnp.exp(lg[i] * ((row + 1.0) if chains[i][1] else (c - row))) for i in ids]
    zeta = [jnp.exp(lg[i] * ((c - 1.0 - row) if chains[i][1] else row)) for i in ids]
    qx = [(qkv(i)[0].astype(F32) * xi[i]).astype(BF16) for i in ids]
    kz = [(qkv(i)[1].astype(F32) * zeta[i]).T.astype(BF16) for i in ids]
    state = [state_ref[i] for i in ids]
    out = [jnp.dot(scores[i], qkv(i)[2], preferred_element_type=F32)
           + jnp.dot(qx[i], state[i].astype(BF16), preferred_element_type=F32) for i in ids]
    for i, (h, fwd) in enumerate(chains):
        o_ref = of_ref if fwd else ob_ref
        o_ref[:, h * RET_DV:(h + 1) * RET_DV] = out[i].astype(o_ref.dtype)
        state_ref[i] = jnp.exp(lg[i] * c) * state[i] + jnp.dot(kz[i], qkv(i)[2], preferred_element_type=F32)


def _retention(proj, decay_fwd, decay_bwd, bsz, seq):
    c = RET_CHUNK
    nc = seq // c

    def specs(rev):
        def chunk(b, n):
            return b * nc + ((nc - 1 - n) if rev else n)
        return [
            pl.BlockSpec((c, RET_QK_W), lambda b, n: (chunk(b, n), 0)),
            pl.BlockSpec((c, RET_QK_W), lambda b, n: (chunk(b, n), 1)),
            pl.BlockSpec((c, RET_V_W), lambda b, n: (chunk(b, n), 2 * RET_QK_W // RET_V_W)),
        ], pl.BlockSpec((c, RET_V_W), lambda b, n: (chunk(b, n), 0))

    in_f, out_f = specs(False)
    in_b, out_b = specs(True)
    smem = pl.BlockSpec(memory_space=pltpu.SMEM)
    out = jax.ShapeDtypeStruct((bsz * seq, RET_V_W), BF16)
    n_chains = 2 * RET_HEADS
    return pl.pallas_call(
        _retention_kernel,
        out_shape=(out, out),
        grid=(bsz, nc),
        in_specs=[smem, smem] + in_f + in_b,
        out_specs=(out_f, out_b),
        scratch_shapes=[pltpu.VMEM((n_chains, RET_DK, RET_DV), F32), pltpu.VMEM((n_chains, c, c), F32)],
        compiler_params=_cparams(("parallel", "arbitrary")),
        name="retention",
    )(decay_fwd, decay_bwd, proj, proj, proj, proj, proj, proj)


def _split_bf16(a):
    hi = a.astype(BF16)
    lo = (a - hi.astype(F32)).astype(BF16)
    return hi, lo


def _finish_out_proj(x_ref, mix, cross_ref, w_ref, g2_ref, wr_ref, xo_ref, h2_ref):
    mix_w = mix.shape[1]
    d = x_ref.shape[1]
    y = jnp.dot(mix, w_ref[0, :mix_w, :], preferred_element_type=F32)
    y = y + jnp.dot(cross_ref[...], w_ref[0, mix_w:, :], preferred_element_type=F32)
    x = x_ref[...] + y
    xo_ref[...] = x
    ms = jnp.mean(x * x, axis=-1, keepdims=True)
    h2 = x * lax.rsqrt(ms + NORM_EPS) * g2_ref[...]
    h2_ref[:, 0, :d] = h2
    h_hi, h_lo = _split_bf16(h2)
    w_hi, w_lo = _split_bf16(wr_ref[...])
    both = jnp.dot(h_hi, jnp.concatenate([w_hi, w_lo], axis=1), preferred_element_type=F32)
    logits = both[:, :LANES] + both[:, LANES:] + jnp.dot(h_lo, w_hi, preferred_element_type=F32)
    lane = lax.broadcasted_iota(jnp.int32, logits.shape, 1)
    logits = jnp.where(lane < N_EXPERTS, logits, NEG_INF)
    m = jnp.max(logits, axis=-1, keepdims=True)
    p = jnp.exp(logits - m)
    h2_ref[:, 0, d:] = p / jnp.sum(p, axis=-1, keepdims=True)


def _out_proj_attn_kernel(x_ref, mix_ref, cross_ref, w_ref, g2_ref, wr_ref, xo_ref, h2_ref):
    _finish_out_proj(x_ref, mix_ref[...], cross_ref, w_ref, g2_ref, wr_ref, xo_ref, h2_ref)


def _out_proj_ret_kernel(x_ref, fwd_ref, bwd_ref, gate_ref, og_ref, cross_ref, w_ref, g2_ref, wr_ref,
                         xo_ref, h2_ref):
    parts = []
    for h in range(RET_HEADS):
        hs = slice(h * RET_DV, (h + 1) * RET_DV)
        y = fwd_ref[:, hs].astype(F32) + bwd_ref[:, hs].astype(F32)
        ms = jnp.mean(y * y, axis=-1, keepdims=True)
        y = y * lax.rsqrt(ms + NORM_EPS) * og_ref[:, hs]
        g = gate_ref[:, hs].astype(F32)
        parts.append((g * jax.nn.sigmoid(g) * y).astype(BF16))
    mix = jnp.concatenate(parts, axis=-1)
    _finish_out_proj(x_ref, mix, cross_ref, w_ref, g2_ref, wr_ref, xo_ref, h2_ref)


def _out_proj(x, mix_inputs, cross, w_out, layer, g2, w_router, *, tm=512):
    m, d = x.shape
    row = lambda i: (i, 0)
    const = lambda i: (0, 0)
    tail_specs = [
        pl.BlockSpec((tm, CROSS_W), row),
        pl.BlockSpec((1, d, d), lambda i: (layer, 0, 0)),
        pl.BlockSpec((1, d), const),
        pl.BlockSpec((d, LANES), const),
    ]
    w_router = jnp.pad(w_router, ((0, 0), (0, LANES - N_EXPERTS)))
    tail_args = (cross, w_out, g2.reshape(1, d), w_router)
    if len(mix_inputs) == 1:
        kernel = _out_proj_attn_kernel
        specs = [pl.BlockSpec((tm, d), row), pl.BlockSpec((tm, ATTN_Q_W), row)] + tail_specs
        args = (x, mix_inputs[0]) + tail_args
    else:
        fwd, bwd, proj, gate_col, out_gain = mix_inputs
        kernel = _out_proj_ret_kernel
        specs = [
            pl.BlockSpec((tm, d), row),
            pl.BlockSpec((tm, RET_V_W), row),
            pl.BlockSpec((tm, RET_V_W), row),
            pl.BlockSpec((tm, RET_V_W), lambda i: (i, gate_col)),
            pl.BlockSpec((1, RET_V_W), const),
        ] + tail_specs
        args = (x, fwd, bwd, proj, out_gain.reshape(1, RET_V_W)) + tail_args
    return pl.pallas_call(
        kernel,
        out_shape=(jax.ShapeDtypeStruct((m, d), F32), jax.ShapeDtypeStruct((m, 1, d + LANES), F32)),
        grid=(m // tm,),
        in_specs=specs,
        out_specs=(pl.BlockSpec((tm, d), row), pl.BlockSpec((tm, 1, d + LANES), lambda i: (i, 0, 0))),
        compiler_params=_cparams(("parallel",)),
        name="out_proj",
    )(*args)


def _route_kernel(aff_ref, idx_ref, pos_ref, cum_ref, *, cap):
    aff = aff_ref[0]
    e, nt, _ = aff.shape
    rows = e * nt
    capf = float(cap)

    def count(mask):
        c = jnp.sum(jnp.where(mask, 1.0, 0.0), axis=1, keepdims=True)
        return jnp.sum(c, axis=2, keepdims=True)

    def search(_, lohi):
        lo, hi = lohi
        mid = lo + ((hi - lo + 1) >> 1)
        ok = count(aff >= pltpu.bitcast(mid, F32)) >= capf
        return jnp.where(ok, mid, lo), jnp.where(ok, hi, mid - 1)

    lo0 = jnp.zeros((e, 1, 1), jnp.int32)
    hi0 = jnp.full((e, 1, 1), 0x7F800000, jnp.int32)
    thr_bits, _ = lax.fori_loop(0, 31, search, (lo0, hi0))
    thr = pltpu.bitcast(thr_bits, F32)

    ri = lax.broadcasted_iota(jnp.int32, (LANES, LANES), 0)
    ci = lax.broadcasted_iota(jnp.int32, (LANES, LANES), 1)
    tri = jnp.where(ri <= ci, 1.0, 0.0).astype(BF16)
    shift = nt.bit_length() - 1
    rr = lax.broadcasted_iota(jnp.int32, (rows, rows), 0)
    cc = lax.broadcasted_iota(jnp.int32, (rows, rows), 1)
    earlier_tile = jnp.where(((rr >> shift) == (cc >> shift)) & (cc < rr), 1.0, 0.0).astype(BF16)

    def prefix(m3):
        m2 = m3.reshape(rows, LANES).astype(BF16)
        inc = jnp.dot(m2, tri, preferred_element_type=F32)
        tot = jnp.broadcast_to(inc[:, LANES - 1:LANES], (rows, LANES)).astype(BF16)
        offs = jnp.dot(earlier_tile, tot, preferred_element_type=F32)
        return (inc + offs).reshape(e, nt, LANES)

    gt = aff > thr
    eq = aff == thr
    need = capf - count(gt)
    eqf = jnp.where(eq, 1.0, 0.0)
    tie_rank = prefix(eqf) - eqf
    sel = gt | (eq & (tie_rank < need))
    self_ = jnp.where(sel, 1.0, 0.0)
    cum = prefix(self_)
    pos = (cum - self_).astype(jnp.int32)
    pos_ref[0] = jnp.where(sel, pos, -1 - pos)
    cum_ref[...] = cum

    nt_dims = (((1,), (1,)), ((), ()))
    kcol = lax.broadcasted_iota(jnp.int32, (cap, 1), 0).astype(F32)
    lane8 = lax.broadcasted_iota(jnp.int32, (8, LANES), 1)
    pick_last = jnp.where(lane8 == LANES - 1, 1.0, 0.0).astype(BF16)
    ones_nt = jnp.ones((8, nt), BF16)
    ones_ln = jnp.ones((8, LANES), BF16)

    def expert_list(ex, carry):
        cum_e = cum_ref[ex]
        hi = jnp.floor(cum_e * (1.0 / 32.0))
        lo = cum_e - 32.0 * hi
        hib, lob = hi.astype(BF16), lo.astype(BF16)
        tile_end = (32.0 * lax.dot_general(pick_last, hib, nt_dims, preferred_element_type=F32)
                    + lax.dot_general(pick_last, lob, nt_dims, preferred_element_type=F32))
        done = jnp.where(tile_end[0:1, :] <= kcol, 1.0, 0.0).astype(BF16)
        full = lax.dot_general(ones_nt, done, nt_dims, preferred_element_type=F32)
        step_hi = (pltpu.roll(hi, nt - 1, 0) - hi).astype(BF16)
        step_lo = (pltpu.roll(lo, nt - 1, 0) - lo).astype(BF16)
        sel_hi = jnp.dot(done, step_hi, preferred_element_type=F32) + hi[0:1, :]
        sel_lo = jnp.dot(done, step_lo, preferred_element_type=F32) + lo[0:1, :]
        inside = jnp.where(32.0 * sel_hi + sel_lo <= kcol, 1.0, 0.0).astype(BF16)
        part = lax.dot_general(ones_ln, inside, nt_dims, preferred_element_type=F32)
        idx_ref[0, pl.ds(pl.multiple_of(ex * 8, 8), 8), :] = (LANES * full + part).astype(jnp.int32)
        return carry

    lax.fori_loop(0, e, expert_list, 0)


def _route(aff_t, cap):
    bsz, e, s = aff_t.shape
    nt = s // LANES
    assert nt & (nt - 1) == 0
    blk = pl.BlockSpec((1, e, nt, LANES), lambda b: (b, 0, 0, 0))
    idx8, pos = pl.pallas_call(
        functools.partial(_route_kernel, cap=cap),
        out_shape=(jax.ShapeDtypeStruct((bsz, e * 8, cap), jnp.int32),
                   jax.ShapeDtypeStruct((bsz, e, nt, LANES), jnp.int32)),
        grid=(bsz,),
        in_specs=[blk],
        out_specs=(pl.BlockSpec((1, e * 8, cap), lambda b: (b, 0, 0)), blk),
        scratch_shapes=[pltpu.VMEM((e, nt, LANES), F32)],
        compiler_params=_cparams(("parallel",)),
        name="route",
    )(aff_t.reshape(bsz, e, nt, LANES))
    return idx8[:, ::8, :], pos.reshape(bsz, e, s)


def _expert_kernel(idx_ref, h_hbm, wg_hbm, wu_hbm, wd_hbm, y_ref, xa, xb, sem, wg_s, wu_s, wd_s, stg_a, stg_d,
                   wsem, *, layer, cap, d):
    e = pl.program_id(0)
    b = pl.program_id(1)
    r = pl.program_id(2)
    n_e = pl.num_programs(0)
    n_b = pl.num_programs(1)
    rows = xa.shape[0]
    step = (e * n_b + b) * 2 + r
    n_steps = n_e * n_b * 2
    sub = b * 2 + r
    n_sub = n_b * 2
    ca = stg_a.shape[1]
    cd = stg_d.shape[1]

    def pieces(s):
        out = []
        for j in range(2):
            ra = pl.multiple_of((2 * s + j) * ca, ca)
            rd = pl.multiple_of((2 * s + j) * cd, cd)
            out.append((wg_hbm, wg_s, ra, ca, stg_a.at[j]))
            out.append((wu_hbm, wu_s, ra, ca, stg_a.at[2 + j]))
            out.append((wd_hbm, wd_s, rd, cd, stg_d.at[j]))
        return out

    def stage_copies(ee, s):
        return [pltpu.make_async_copy(src.at[layer, ee, pl.ds(r0, n), :], stg, wsem.at[0])
                for src, _, r0, n, stg in pieces(s)]

    def convert(dst_set, s):
        for _, dst, r0, n, stg in pieces(s):
            dst[dst_set, pl.ds(r0, n), :] = stg[...].astype(BF16)

    next_expert = lambda ee: jnp.where(ee == n_e - 1, 0, ee + 1)
    cur_set = e % 2

    def gather(bb, ee, rr, buf, slot, unrolled):
        base = (bb * n_e + ee) * cap + rr * rows

        def start(i):
            pltpu.make_async_copy(h_hbm.at[idx_ref[base + i]], buf.at[pl.ds(i, 1), :], sem.at[slot]).start()

        if unrolled:
            for i in range(rows):
                start(i)
        else:
            def body(i, carry):
                start(i)
                return carry
            lax.fori_loop(0, rows, body, 0)

    def wait(buf, slot):
        pltpu.make_async_copy(h_hbm.at[pl.ds(0, rows), 0, :], buf, sem.at[slot]).wait()

    wrap_b = r == 1
    wrap_e = wrap_b & (b == n_b - 1)
    b1 = jnp.where(wrap_b, jnp.where(b == n_b - 1, 0, b + 1), b)
    e1 = jnp.where(wrap_e, jnp.where(e == n_e - 1, 0, e + 1), e)

    @pl.when(step == 0)
    def _():
        gather(b, e, r, xa, 0, False)
        for s in range(n_sub):
            for cp in stage_copies(0, s):
                cp.start()
            for cp in stage_copies(0, s):
                cp.wait()
            convert(0, s)
        for cp in stage_copies(next_expert(e), sub):
            cp.start()

    def compute(cur, cur_slot, nxt, nxt_slot):
        gather(b1, e1, 1 - r, nxt, nxt_slot, True)
        wait(cur, cur_slot)
        x = cur[:, :d].astype(BF16)
        hg = jnp.dot(x, wg_s[cur_set], preferred_element_type=F32)
        for cp in stage_copies(next_expert(e), sub):
            cp.wait()
        convert(1 - cur_set, sub)
        hu = jnp.dot(x, wu_s[cur_set], preferred_element_type=F32)
        hid = (hg * jax.nn.sigmoid(hg) * hu).astype(BF16)
        lane = lax.broadcasted_iota(jnp.int32, (rows, LANES), 1)
        gate = jnp.sum(jnp.where(lane == e, cur[:, d:], 0.0), axis=-1, keepdims=True)
        half = d // 2
        for c in range(2):
            cols = slice(c * half, (c + 1) * half)
            y = jnp.dot(hid, wd_s[cur_set, :, cols], preferred_element_type=F32)
            y_ref[0, 0, :, cols] = (y * gate).astype(y_ref.dtype)

        @pl.when(step < n_steps - 1)
        def _():
            last_sub = sub == n_sub - 1
            e_next = jnp.where(last_sub, e + 1, e)
            for cp in stage_copies(next_expert(e_next), jnp.where(last_sub, 0, sub + 1)):
                cp.start()

    @pl.when(r == 0)
    def _():
        compute(xa, 0, xb, 1)

    @pl.when(r == 1)
    def _():
        compute(xb, 1, xa, 0)

    @pl.when(step == n_steps - 1)
    def _():
        wait(xa, 0)


def _experts(idx, h2, wg, wu, wd, layer):
    bsz, e, cap = idx.shape
    d = h2.shape[2] - LANES
    ff = wg.shape[3]
    rows = cap // 2
    n_pieces = 2 * 2 * bsz
    any_space = pl.BlockSpec(memory_space=pl.ANY)
    grid_spec = pltpu.PrefetchScalarGridSpec(
        num_scalar_prefetch=1,
        grid=(e, bsz, 2),
        in_specs=[any_space, any_space, any_space, any_space],
        out_specs=pl.BlockSpec((1, 1, rows, d), lambda ei, b, r, idx: (b, ei, r, 0)),
        scratch_shapes=[pltpu.VMEM((rows, d + LANES), F32), pltpu.VMEM((rows, d + LANES), F32),
                        pltpu.SemaphoreType.DMA((2,)),
                        pltpu.VMEM((2, d, ff), BF16), pltpu.VMEM((2, d, ff), BF16), pltpu.VMEM((2, ff, d), BF16),
                        pltpu.VMEM((4, d // n_pieces, ff), F32), pltpu.VMEM((2, ff // n_pieces, d), F32),
                        pltpu.SemaphoreType.DMA((1,))],
    )
    return pl.pallas_call(
        functools.partial(_expert_kernel, layer=layer, cap=cap, d=d),
        out_shape=jax.ShapeDtypeStruct((bsz, e, cap, d), BF16),
        grid_spec=grid_spec,
        compiler_params=pltpu.CompilerParams(dimension_semantics=("arbitrary", "arbitrary", "arbitrary"),
                                             vmem_limit_bytes=EXPERT_VMEM_LIMIT),
        name="experts",
    )(idx.reshape(-1), h2, wg, wu, wd)


def _combine_kernel(off_ref, x_ref, pos_ref, y_hbm, o_ref, ybuf, sem, *, cap, n_tiles):
    b = pl.program_id(0)
    i = pl.program_id(1)
    n_b = pl.num_programs(0)
    tt = x_ref.shape[0]
    n_exp = pos_ref.shape[1]
    ch = COMBINE_CHUNK
    per_tile = LANES // ch
    step = b * n_tiles + i
    slot = step % 2
    last_start = cap - ch

    def tile_rows(bb, ii):
        base = (bb * (n_tiles + 1) + ii) * n_exp
        first = [(off_ref[base + ex] // COMBINE_ALIGN) * COMBINE_ALIGN for ex in range(n_exp)]
        end = [off_ref[base + n_exp + ex] for ex in range(n_exp)]
        return first, end

    def chunk_start(lo):
        return pl.multiple_of(jnp.minimum(lo, last_start), COMBINE_ALIGN)

    def chunk_copy(bb, ex, lo, sl):
        return pltpu.make_async_copy(y_hbm.at[bb, ex, pl.ds(chunk_start(lo), ch), :],
                                     ybuf.at[sl, pl.ds(ex * ch, ch), :], sem.at[sl])

    def fetch(bb, los, sl):
        for ex in range(n_exp):
            chunk_copy(bb, ex, los[ex], sl).start()

    def wait(sl):
        for ex in range(n_exp):
            chunk_copy(0, ex, 0, sl).wait()

    lane = lax.broadcasted_iota(jnp.int32, (tt, LANES), 1)

    def onehot(los):
        blocks = []
        for j in range(n_exp // per_tile):
            target = jnp.full((tt, LANES), -1, jnp.int32)
            for u in range(per_tile):
                ex = j * per_tile + u
                p = pos_ref[:, ex:ex + 1]
                lo = los[ex]
                q = jnp.where((p >= lo) & (p < lo + ch), p - chunk_start(lo) + u * ch, -1)
                target = jnp.where((lane >= u * ch) & (lane < (u + 1) * ch), q, target)
            blocks.append(jnp.where(target == lane, 1.0, 0.0).astype(BF16))
        return jnp.concatenate(blocks, axis=1)

    first, end = tile_rows(b, i)

    @pl.when(step == 0)
    def _():
        fetch(b, first, slot)

    @pl.when(step < n_b * n_tiles - 1)
    def _():
        wrap = i == n_tiles - 1
        nxt, _ = tile_rows(jnp.where(wrap, b + 1, b), jnp.where(wrap, 0, i + 1))
        fetch(jnp.where(wrap, b + 1, b), nxt, 1 - slot)

    wait(slot)
    o_ref[...] = x_ref[...] + jnp.dot(onehot(first), ybuf[slot], preferred_element_type=F32)

    n_pass = functools.reduce(jnp.maximum, [(end[ex] - first[ex] + ch - 1) // ch for ex in range(n_exp)])

    def extra(c, carry):
        los = [first[ex] + c * ch for ex in range(n_exp)]
        fetch(b, los, slot)
        wait(slot)
        o_ref[...] += jnp.dot(onehot(los), ybuf[slot], preferred_element_type=F32)
        return carry

    lax.fori_loop(1, n_pass, extra, 0)


def _combine(x, pos, offsets, y, bsz, seq):
    d = x.shape[1]
    e, cap = y.shape[1], y.shape[2]
    tt = COMBINE_TOKENS
    n_tiles = seq // tt
    row = lambda b, i, off: (b * n_tiles + i, 0)
    grid_spec = pltpu.PrefetchScalarGridSpec(
        num_scalar_prefetch=1,
        grid=(bsz, n_tiles),
        in_specs=[
            pl.BlockSpec((tt, d), row),
            pl.BlockSpec((tt, e), row),
            pl.BlockSpec(memory_space=pl.ANY),
        ],
        out_specs=pl.BlockSpec((tt, d), row),
        scratch_shapes=[pltpu.VMEM((2, e * COMBINE_CHUNK, d), BF16), pltpu.SemaphoreType.DMA((2,))],
    )
    return pl.pallas_call(
        functools.partial(_combine_kernel, cap=cap, n_tiles=n_tiles),
        out_shape=jax.ShapeDtypeStruct(x.shape, F32),
        grid_spec=grid_spec,
        compiler_params=_cparams(("arbitrary", "arbitrary")),
        name="combine",
    )(offsets.reshape(-1), x, pos, y)


def _moe(x, h2, wg, wu, wd, layer, bsz, seq):
    d = x.shape[1]
    cap = CAPACITY_FACTOR * seq // N_EXPERTS
    aff_t = jnp.swapaxes(h2[:, 0, d:d + N_EXPERTS].reshape(bsz, seq, N_EXPERTS), 1, 2)
    idx, pos_t = _route(aff_t, cap)
    rows_of = idx + (jnp.arange(bsz, dtype=jnp.int32) * seq)[:, None, None]
    y = _experts(rows_of, h2, wg, wu, wd, layer)
    first = pos_t[:, :, ::COMBINE_TOKENS]
    offsets = jnp.concatenate(
        [jnp.where(first >= 0, first, -1 - first), jnp.full((bsz, N_EXPERTS, 1), cap, jnp.int32)], axis=-1)
    offsets = jnp.swapaxes(offsets, 1, 2)
    pos = jnp.swapaxes(pos_t, 1, 2).reshape(bsz * seq, N_EXPERTS)
    return _combine(x, pos, offsets, y, bsz, seq)


def kernel(x, mem, norm1_gain, w_in_attn, attn_q_gain, attn_k_gain, attn_sink, w_in_ret, ret_decay_fwd,
           ret_decay_bwd, ret_out_gain, mem_norm_gain, w_mem_kv, cross_q_gain, cross_k_gain, w_out,
           norm2_gain, w_router, w_gate, w_up, w_down):
    bsz, seq, d = x.shape
    mem_len = mem.shape[1]
    depth = norm1_gain.shape[0]
    xf = x.reshape(bsz * seq, d)
    memf = mem.reshape(bsz * mem_len, d)
    scale = HEAD_DIM ** -0.5
    ones = lambda n: jnp.ones((n,), F32)
    tile = lambda g, n: jnp.tile(g, n)
    w_in_attn, w_in_ret, w_mem_kv, w_out = (w.astype(BF16) for w in (w_in_attn, w_in_ret, w_mem_kv, w_out))
    for layer in range(depth):
        li = layer // 2
        kv_scale = jnp.concatenate([tile(cross_k_gain[layer], CROSS_HEADS), ones(CROSS_W)])
        kvm = _norm_matmul(memf, mem_norm_gain[layer], w_mem_kv, layer, kv_scale,
                           (0, CROSS_W, 0, 0), tm=bsz * mem_len)
        qc_scale = tile(cross_q_gain[layer] * scale, CROSS_HEADS)
        if layer % 2 == 0:
            post = jnp.concatenate([tile(attn_q_gain[li] * scale, ATTN_Q_HEADS),
                                    tile(attn_k_gain[li], ATTN_KV_HEADS), ones(ATTN_KV_W), qc_scale])
            q_end = ATTN_Q_W + ATTN_KV_W
            qc_start = q_end + ATTN_KV_W
            proj = _norm_matmul(xf, norm1_gain[layer], w_in_attn, li, post,
                                (0, q_end, qc_start, qc_start + CROSS_W), tm=1024)
            mix_inputs = (_win_attn(proj, attn_sink[li], bsz, seq),)
        else:
            post = jnp.concatenate([ones(RET_QK_W), jnp.full((RET_QK_W,), RET_DK ** -0.5, F32),
                                    ones(2 * RET_V_W), qc_scale])
            qc_start = 2 * RET_QK_W + 2 * RET_V_W
            proj = _norm_matmul(xf, norm1_gain[layer], w_in_ret, li, post,
                                (qc_start, qc_start + CROSS_W, 0, 0), tm=1024)
            fwd, bwd = _retention(proj, ret_decay_fwd[li], ret_decay_bwd[li], bsz, seq)
            gate_col = (2 * RET_QK_W + RET_V_W) // RET_V_W
            mix_inputs = (fwd, bwd, proj, gate_col, ret_out_gain[li])
        cross = _cross_attn(proj, qc_start // CROSS_W, kvm, bsz, seq, mem_len)
        xf, h2 = _out_proj(xf, mix_inputs, cross, w_out, layer, norm2_gain[layer], w_router[layer])
        xf = _moe(xf, h2, w_gate, w_up, w_down, layer, bsz, seq)
    return xf.reshape(bsz, seq, d)
```

```python
import functools
import math

import jax
import jax.numpy as jnp
import numpy as np
from jax import lax
from jax.experimental import pallas as pl
from jax.experimental.pallas import tpu as pltpu

F32 = jnp.float32
BF16 = jnp.bfloat16

HEAD_DIM = 128
ATTN_Q_HEADS = 12
ATTN_KV_HEADS = 4
ATTN_GROUP = ATTN_Q_HEADS // ATTN_KV_HEADS
WINDOW = 128
RET_HEADS = 6
RET_DK = 128
RET_DV = 256
CROSS_HEADS = 4
N_EXPERTS = 16
CAPACITY_FACTOR = 2
NORM_EPS = 1e-6
NEG_INF = -1e30

ATTN_Q_W = ATTN_Q_HEADS * HEAD_DIM
ATTN_KV_W = ATTN_KV_HEADS * HEAD_DIM
CROSS_W = CROSS_HEADS * HEAD_DIM
RET_QK_W = RET_HEADS * RET_DK
RET_V_W = RET_HEADS * RET_DV

LANES = 128
VMEM_LIMIT = 56 * 1024 * 1024
EXPERT_VMEM_LIMIT = 60 * 1024 * 1024

RET_CHUNK = 256
COMBINE_TOKENS = 256
COMBINE_CHUNK = 64
COMBINE_ALIGN = 16


def _cparams(semantics):
    return pltpu.CompilerParams(dimension_semantics=semantics, vmem_limit_bytes=VMEM_LIMIT)


def _alibi_slopes(n):
    def pow2_slopes(m):
        start = 2.0 ** (-(2.0 ** -(math.log2(m) - 3)))
        return [start ** (i + 1) for i in range(m)]
    if math.log2(n).is_integer():
        return pow2_slopes(n)
    closest = 2 ** math.floor(math.log2(n))
    return pow2_slopes(closest) + _alibi_slopes(2 * closest)[0::2][: n - closest]


def _norm_matmul_kernel(x_ref, g_ref, w_ref, ps_ref, o_ref, h_ref, *, norm_tiles):
    j = pl.program_id(1)

    @pl.when(j == 0)
    def _():
        x = x_ref[...]
        ms = jnp.mean(x * x, axis=-1, keepdims=True)
        h_ref[...] = (x * lax.rsqrt(ms + NORM_EPS) * g_ref[...]).astype(BF16)

    acc = jnp.dot(h_ref[...], w_ref[0], preferred_element_type=F32)
    tn = acc.shape[1]
    lo1, hi1, lo2, hi2 = norm_tiles
    normed = ((j >= lo1) & (j < hi1)) | ((j >= lo2) & (j < hi2))

    @pl.when(normed)
    def _():
        for c in range(tn // HEAD_DIM):
            sl = slice(c * HEAD_DIM, (c + 1) * HEAD_DIM)
            a = acc[:, sl]
            ms = jnp.mean(a * a, axis=-1, keepdims=True)
            o_ref[:, sl] = (a * lax.rsqrt(ms + NORM_EPS) * ps_ref[:, sl]).astype(o_ref.dtype)

    @pl.when(jnp.logical_not(normed))
    def _():
        o_ref[...] = (acc * ps_ref[...]).astype(o_ref.dtype)


def _norm_matmul(x, gain, w, layer, post_scale, norm_cols, *, tm, tn=512):
    m, d = x.shape
    n = w.shape[2]
    assert m % tm == 0 and n % tn == 0 and all(c % tn == 0 for c in norm_cols)
    norm_tiles = tuple(c // tn for c in norm_cols)
    return pl.pallas_call(
        functools.partial(_norm_matmul_kernel, norm_tiles=norm_tiles),
        out_shape=jax.ShapeDtypeStruct((m, n), BF16),
        grid=(m // tm, n // tn),
        in_specs=[
            pl.BlockSpec((tm, d), lambda i, j: (i, 0)),
            pl.BlockSpec((1, d), lambda i, j: (0, 0)),
            pl.BlockSpec((1, d, tn), lambda i, j: (layer, 0, j)),
            pl.BlockSpec((1, tn), lambda i, j: (0, j)),
        ],
        out_specs=pl.BlockSpec((tm, tn), lambda i, j: (i, j)),
        scratch_shapes=[pltpu.VMEM((tm, d), BF16)],
        compiler_params=_cparams(("parallel", "arbitrary")),
        name="norm_matmul",
    )(x, gain.reshape(1, d), w, post_scale.reshape(1, n))


def _win_attn_kernel(sink_ref, bias_ref, q_ref, k0_ref, k1_ref, k2_ref, k3_ref, v0_ref, v1_ref, v2_ref, v3_ref,
                     o_ref):
    n = pl.program_id(1)
    nsteps = pl.num_programs(1)
    c = WINDOW
    k_refs = (k0_ref, k1_ref, k2_ref, k3_ref)
    v_refs = (v0_ref, v1_ref, v2_ref, v3_ref)
    col = lax.broadcasted_iota(jnp.int32, (1, 3 * c), 1)
    present = [col >= jnp.where(n > 0, 0, c), col < jnp.where(n < nsteps - 1, 3 * c, 2 * c)]
    nt_dims = (((1,), (1,)), ((), ()))
    chains = [(u, h) for u in range(2) for h in range(ATTN_KV_HEADS)]

    def band(refs, u, h):
        hs = slice(h * HEAD_DIM, (h + 1) * HEAD_DIM)
        return jnp.concatenate([refs[u + j][:, hs] for j in range(3)], axis=0)

    def group_rows(h, make):
        return jnp.concatenate([make(h * ATTN_GROUP + g) for g in range(ATTN_GROUP)], axis=0)

    kb = [band(k_refs, u, h) for u, h in chains]
    ones = jnp.ones((3 * c, HEAD_DIM), BF16)
    vb = [jnp.concatenate([band(v_refs, u, h), ones], axis=1) for u, h in chains]
    q = [group_rows(h, lambda qh: q_ref[u * c:(u + 1) * c, qh * HEAD_DIM:(qh + 1) * HEAD_DIM]) for u, h in chains]
    sink = [group_rows(h, lambda qh: jnp.full((c, HEAD_DIM), sink_ref[qh], F32)) for u, h in chains]
    s = [lax.dot_general(q[i], kb[i], nt_dims, preferred_element_type=F32) for i in range(len(chains))]
    logits = [jnp.where(present[u], s[i] + bias_ref[h], NEG_INF) for i, (u, h) in enumerate(chains)]
    tile_max = [jnp.maximum(jnp.maximum(l[:, :c], l[:, c:2 * c]), l[:, 2 * c:]) for l in logits]
    m = [jnp.maximum(jnp.broadcast_to(jnp.max(tile_max[i], axis=-1, keepdims=True), sink[i].shape), sink[i])
         for i in range(len(chains))]
    p = [jnp.concatenate([jnp.exp(logits[i][:, j * c:(j + 1) * c] - m[i]) for j in range(3)], axis=1).astype(BF16)
         for i in range(len(chains))]
    pv = [jnp.dot(p[i], vb[i], preferred_element_type=F32) for i in range(len(chains))]
    o = [pv[i][:, :HEAD_DIM] / (pv[i][:, HEAD_DIM:] + jnp.exp(sink[i] - m[i])) for i in range(len(chains))]
    for i, (u, h) in enumerate(chains):
        for g in range(ATTN_GROUP):
            qh = h * ATTN_GROUP + g
            o_ref[u * c:(u + 1) * c, qh * HEAD_DIM:(qh + 1) * HEAD_DIM] = (
                o[i][g * c:(g + 1) * c].astype(o_ref.dtype))


def _alibi_band_bias():
    c = WINDOW
    dist = np.abs(np.arange(c)[:, None] + c - np.arange(3 * c)[None, :])
    slopes = np.array(_alibi_slopes(ATTN_Q_HEADS), np.float32)
    bias = np.where(dist <= WINDOW, -slopes[:, None, None] * dist.astype(np.float32)[None], np.float32(NEG_INF))
    return jnp.asarray(bias.astype(np.float32).reshape(ATTN_KV_HEADS, ATTN_GROUP * c, 3 * c))


def _win_attn(proj, sink, bsz, seq):
    c = WINDOW
    nb = seq // c
    kcol = ATTN_Q_W // ATTN_KV_W
    vcol = kcol + 1

    nsteps = nb // 2

    def kv_spec(col, j):
        def imap(b, n):
            return (b * nb + jnp.clip(2 * n - 1 + j, 0, nb - 1), col)
        return pl.BlockSpec((c, ATTN_KV_W), imap)

    q_spec = pl.BlockSpec((2 * c, ATTN_Q_W), lambda b, n: (b * nsteps + n, 0))
    return pl.pallas_call(
        _win_attn_kernel,
        out_shape=jax.ShapeDtypeStruct((bsz * seq, ATTN_Q_W), BF16),
        grid=(bsz, nsteps),
        in_specs=[
            pl.BlockSpec(memory_space=pltpu.SMEM),
            pl.BlockSpec((ATTN_KV_HEADS, ATTN_GROUP * c, 3 * c), lambda b, n: (0, 0, 0)),
            q_spec,
        ] + [kv_spec(kcol, j) for j in range(4)] + [kv_spec(vcol, j) for j in range(4)],
        out_specs=q_spec,
        compiler_params=_cparams(("parallel", "arbitrary")),
        name="win_attn",
    )(sink, _alibi_band_bias(), *([proj] * 9))


def _cross_attn_kernel(q_ref, kv_ref, o_ref):
    heads = range(CROSS_HEADS)
    mem_len = kv_ref.shape[0]
    nt_dims = (((1,), (1,)), ((), ()))
    hs = [slice(h * HEAD_DIM, (h + 1) * HEAD_DIM) for h in heads]
    ones = jnp.ones((mem_len, HEAD_DIM), BF16)
    vb = [jnp.concatenate([kv_ref[:, CROSS_W + h * HEAD_DIM:CROSS_W + (h + 1) * HEAD_DIM], ones], axis=1)
          for h in heads]
    s = [lax.dot_general(q_ref[:, hs[h]], kv_ref[:, hs[h]], nt_dims, preferred_element_type=F32)
         for h in heads]
    tile_max = [functools.reduce(jnp.maximum, [s[h][:, j:j + LANES] for j in range(0, mem_len, LANES)])
                for h in heads]
    m = [jnp.broadcast_to(jnp.max(tile_max[h], axis=-1, keepdims=True), tile_max[h].shape) for h in heads]
    p = [jnp.concatenate([jnp.exp(s[h][:, j:j + LANES] - m[h]) for j in range(0, mem_len, LANES)],
                         axis=1).astype(BF16) for h in heads]
    pv = [jnp.dot(p[h], vb[h], preferred_element_type=F32) for h in heads]
    for h in heads:
        o_ref[:, hs[h]] = (pv[h][:, :HEAD_DIM] / pv[h][:, HEAD_DIM:]).astype(o_ref.dtype)


def _cross_attn(proj, qcol, kvm, bsz, seq, mem_len, *, tq=512):
    nq = seq // tq
    return pl.pallas_call(
        _cross_attn_kernel,
        out_shape=jax.ShapeDtypeStruct((bsz * seq, CROSS_W), BF16),
        grid=(bsz, nq),
        in_specs=[
            pl.BlockSpec((tq, CROSS_W), lambda b, i: (b * nq + i, qcol)),
            pl.BlockSpec((mem_len, 2 * CROSS_W), lambda b, i: (b, 0)),
        ],
        out_specs=pl.BlockSpec((tq, CROSS_W), lambda b, i: (b * nq + i, 0)),
        compiler_params=_cparams(("parallel", "arbitrary")),
        name="cross_attn",
    )(proj, kvm)


def _log_sigmoid(x):
    return jnp.minimum(x, 0.0) - jnp.log1p(jnp.exp(-jnp.abs(x)))


def _retention_kernel(df_ref, db_ref, qf_ref, kf_ref, vf_ref, qb_ref, kb_ref, vb_ref,
                      of_ref, ob_ref, state_ref, decay_ref):
    n = pl.program_id(1)
    c = qf_ref.shape[0]
    nt_dims = (((1,), (1,)), ((), ()))
    row = lax.broadcasted_iota(jnp.int32, (c, 1), 0).astype(F32)
    chains = [(h, fwd) for h in range(RET_HEADS) for fwd in (True, False)]
    lg = [_log_sigmoid(jnp.full((1, 1), (df_ref if fwd else db_ref)[h], F32)) for h, fwd in chains]

    @pl.when(n == 0)
    def _():
        state_ref[...] = jnp.zeros_like(state_ref)
        ti = lax.broadcasted_iota(jnp.int32, (c, c), 0)
        si = lax.broadcasted_iota(jnp.int32, (c, c), 1)
        for i, (h, fwd) in enumerate(chains):
            delta = (ti - si) if fwd else (si - ti)
            mask = (delta >= 0) if fwd else (delta > 0)
            decay_ref[i] = jnp.where(mask, jnp.exp(lg[i] * jnp.maximum(delta, 0).astype(F32)), 0.0)

    def qkv(i):
        h, fwd = chains[i]
        q_ref, k_ref, v_ref = (qf_ref, kf_ref, vf_ref) if fwd else (qb_ref, kb_ref, vb_ref)
        return (q_ref[:, h * RET_DK:(h + 1) * RET_DK], k_ref[:, h * RET_DK:(h + 1) * RET_DK],
                v_ref[:, h * RET_DV:(h + 1) * RET_DV])

    ids = range(len(chains))
    scores = [(lax.dot_general(qkv(i)[0], qkv(i)[1], nt_dims, preferred_element_type=F32)
               * decay_ref[i]).astype(BF16) for i in ids]
    xi = [jnp.exp(lg[i] * ((row + 1.0) if chains[i][1] else (c - row))) for i in ids]
    zeta = [jnp.exp(lg[i] * ((c - 1.0 - row) if chains[i][1] else row)) for i in ids]
    qx = [(qkv(i)[0].astype(F32) * xi[i]).astype(BF16) for i in ids]
    kz = [(qkv(i)[1].astype(F32) * zeta[i]).T.astype(BF16) for i in ids]
    state = [state_ref[i] for i in ids]
    out = [jnp.dot(scores[i], qkv(i)[2], preferred_element_type=F32)
           + jnp.dot(qx[i], state[i].astype(BF16), preferred_element_type=F32) for i in ids]
    for i, (h, fwd) in enumerate(chains):
        o_ref = of_ref if fwd else ob_ref
        o_ref[:, h * RET_DV:(h + 1) * RET_DV] = out[i].astype(o_ref.dtype)
        state_ref[i] = jnp.exp(lg[i] * c) * state[i] + jnp.dot(kz[i], qkv(i)[2], preferred_element_type=F32)


def _retention(proj, decay_fwd, decay_bwd, bsz, seq):
    c = RET_CHUNK
    nc = seq // c

    def specs(rev):
        def chunk(b, n):
            return b * nc + ((nc - 1 - n) if rev else n)
        return [
            pl.BlockSpec((c, RET_QK_W), lambda b, n: (chunk(b, n), 0)),
            pl.BlockSpec((c, RET_QK_W), lambda b, n: (chunk(b, n), 1)),
            pl.BlockSpec((c, RET_V_W), lambda b, n: (chunk(b, n), 2 * RET_QK_W // RET_V_W)),
        ], pl.BlockSpec((c, RET_V_W), lambda b, n: (chunk(b, n), 0))

    in_f, out_f = specs(False)
    in_b, out_b = specs(True)
    smem = pl.BlockSpec(memory_space=pltpu.SMEM)
    out = jax.ShapeDtypeStruct((bsz * seq, RET_V_W), BF16)
    n_chains = 2 * RET_HEADS
    return pl.pallas_call(
        _retention_kernel,
        out_shape=(out, out),
        grid=(bsz, nc),
        in_specs=[smem, smem] + in_f + in_b,
        out_specs=(out_f, out_b),
        scratch_shapes=[pltpu.VMEM((n_chains, RET_DK, RET_DV), F32), pltpu.VMEM((n_chains, c, c), F32)],
        compiler_params=_cparams(("parallel", "arbitrary")),
        name="retention",
    )(decay_fwd, decay_bwd, proj, proj, proj, proj, proj, proj)


def _split_bf16(a):
    hi = a.astype(BF16)
    lo = (a - hi.astype(F32)).astype(BF16)
    return hi, lo


def _finish_out_proj(x_ref, mix, cross_ref, w_ref, g2_ref, wr_ref, xo_ref, h2_ref):
    mix_w = mix.shape[1]
    d = x_ref.shape[1]
    y = jnp.dot(mix, w_ref[0, :mix_w, :], preferred_element_type=F32)
    y = y + jnp.dot(cross_ref[...], w_ref[0, mix_w:, :], preferred_element_type=F32)
    x = x_ref[...] + y
    xo_ref[...] = x
    ms = jnp.mean(x * x, axis=-1, keepdims=True)
    h2 = x * lax.rsqrt(ms + NORM_EPS) * g2_ref[...]
    h2_ref[:, 0, :d] = h2
    h_hi, h_lo = _split_bf16(h2)
    w_hi, w_lo = _split_bf16(wr_ref[...])
    both = jnp.dot(h_hi, jnp.concatenate([w_hi, w_lo], axis=1), preferred_element_type=F32)
    logits = both[:, :LANES] + both[:, LANES:] + jnp.dot(h_lo, w_hi, preferred_element_type=F32)
    lane = lax.broadcasted_iota(jnp.int32, logits.shape, 1)
    logits = jnp.where(lane < N_EXPERTS, logits, NEG_INF)
    m = jnp.max(logits, axis=-1, keepdims=True)
    p = jnp.exp(logits - m)
    h2_ref[:, 0, d:] = p / jnp.sum(p, axis=-1, keepdims=True)


def _out_proj_attn_kernel(x_ref, mix_ref, cross_ref, w_ref, g2_ref, wr_ref, xo_ref, h2_ref):
    _finish_out_proj(x_ref, mix_ref[...], cross_ref, w_ref, g2_ref, wr_ref, xo_ref, h2_ref)


def _out_proj_ret_kernel(x_ref, fwd_ref, bwd_ref, gate_ref, og_ref, cross_ref, w_ref, g2_ref, wr_ref,
                         xo_ref, h2_ref):
    parts = []
    for h in range(RET_HEADS):
        hs = slice(h * RET_DV, (h + 1) * RET_DV)
        y = fwd_ref[:, hs].astype(F32) + bwd_ref[:, hs].astype(F32)
        ms = jnp.mean(y * y, axis=-1, keepdims=True)
        y = y * lax.rsqrt(ms + NORM_EPS) * og_ref[:, hs]
        g = gate_ref[:, hs].astype(F32)
        parts.append((g * jax.nn.sigmoid(g) * y).astype(BF16))
    mix = jnp.concatenate(parts, axis=-1)
    _finish_out_proj(x_ref, mix, cross_ref, w_ref, g2_ref, wr_ref, xo_ref, h2_ref)


def _out_proj(x, mix_inputs, cross, w_out, layer, g2, w_router, *, tm=512):
    m, d = x.shape
    row = lambda i: (i, 0)
    const = lambda i: (0, 0)
    tail_specs = [
        pl.BlockSpec((tm, CROSS_W), row),
        pl.BlockSpec((1, d, d), lambda i: (layer, 0, 0)),
        pl.BlockSpec((1, d), const),
        pl.BlockSpec((d, LANES), const),
    ]
    w_router = jnp.pad(w_router, ((0, 0), (0, LANES - N_EXPERTS)))
    tail_args = (cross, w_out, g2.reshape(1, d), w_router)
    if len(mix_inputs) == 1:
        kernel = _out_proj_attn_kernel
        specs = [pl.BlockSpec((tm, d), row), pl.BlockSpec((tm, ATTN_Q_W), row)] + tail_specs
        args = (x, mix_inputs[0]) + tail_args
    else:
        fwd, bwd, proj, gate_col, out_gain = mix_inputs
        kernel = _out_proj_ret_kernel
        specs = [
            pl.BlockSpec((tm, d), row),
            pl.BlockSpec((tm, RET_V_W), row),
            pl.BlockSpec((tm, RET_V_W), row),
            pl.BlockSpec((tm, RET_V_W), lambda i: (i, gate_col)),
            pl.BlockSpec((1, RET_V_W), const),
        ] + tail_specs
        args = (x, fwd, bwd, proj, out_gain.reshape(1, RET_V_W)) + tail_args
    return pl.pallas_call(
        kernel,
        out_shape=(jax.ShapeDtypeStruct((m, d), F32), jax.ShapeDtypeStruct((m, 1, d + LANES), F32)),
        grid=(m // tm,),
        in_specs=specs,
        out_specs=(pl.BlockSpec((tm, d), row), pl.BlockSpec((tm, 1, d + LANES), lambda i: (i, 0, 0))),
        compiler_params=_cparams(("parallel",)),
        name="out_proj",
    )(*args)


def _route_kernel(aff_ref, idx_ref, pos_ref, cum_ref, *, cap):
    aff = aff_ref[0]
    e, nt, _ = aff.shape
    rows = e * nt
    capf = float(cap)

    def count(mask):
        c = jnp.sum(jnp.where(mask, 1.0, 0.0), axis=1, keepdims=True)
        return jnp.sum(c, axis=2, keepdims=True)

    def search(_, lohi):
        lo, hi = lohi
        mid = lo + ((hi - lo + 1) >> 1)
        ok = count(aff >= pltpu.bitcast(mid, F32)) >= capf
        return jnp.where(ok, mid, lo), jnp.where(ok, hi, mid - 1)

    lo0 = jnp.zeros((e, 1, 1), jnp.int32)
    hi0 = jnp.full((e, 1, 1), 0x7F800000, jnp.int32)
    thr_bits, _ = lax.fori_loop(0, 31, search, (lo0, hi0))
    thr = pltpu.bitcast(thr_bits, F32)

    ri = lax.broadcasted_iota(jnp.int32, (LANES, LANES), 0)
    ci = lax.broadcasted_iota(jnp.int32, (LANES, LANES), 1)
    tri = jnp.where(ri <= ci, 1.0, 0.0).astype(BF16)
    shift = nt.bit_length() - 1
    rr = lax.broadcasted_iota(jnp.int32, (rows, rows), 0)
    cc = lax.broadcasted_iota(jnp.int32, (rows, rows), 1)
    earlier_tile = jnp.where(((rr >> shift) == (cc >> shift)) & (cc < rr), 1.0, 0.0).astype(BF16)

    def prefix(m3):
        m2 = m3.reshape(rows, LANES).astype(BF16)
        inc = jnp.dot(m2, tri, preferred_element_type=F32)
        tot = jnp.broadcast_to(inc[:, LANES - 1:LANES], (rows, LANES)).astype(BF16)
        offs = jnp.dot(earlier_tile, tot, preferred_element_type=F32)
        return (inc + offs).reshape(e, nt, LANES)

    gt = aff > thr
    eq = aff == thr
    need = capf - count(gt)
    eqf = jnp.where(eq, 1.0, 0.0)
    tie_rank = prefix(eqf) - eqf
    sel = gt | (eq & (tie_rank < need))
    self_ = jnp.where(sel, 1.0, 0.0)
    cum = prefix(self_)
    pos = (cum - self_).astype(jnp.int32)
    pos_ref[0] = jnp.where(sel, pos, -1 - pos)
    cum_ref[...] = cum

    nt_dims = (((1,), (1,)), ((), ()))
    kcol = lax.broadcasted_iota(jnp.int32, (cap, 1), 0).astype(F32)
    lane8 = lax.broadcasted_iota(jnp.int32, (8, LANES), 1)
    pick_last = jnp.where(lane8 == LANES - 1, 1.0, 0.0).astype(BF16)
    ones_nt = jnp.ones((8, nt), BF16)
    ones_ln = jnp.ones((8, LANES), BF16)

    def expert_list(ex, carry):
        cum_e = cum_ref[ex]
        hi = jnp.floor(cum_e * (1.0 / 32.0))
        lo = cum_e - 32.0 * hi
        hib, lob = hi.astype(BF16), lo.astype(BF16)
        tile_end = (32.0 * lax.dot_general(pick_last, hib, nt_dims, preferred_element_type=F32)
                    + lax.dot_general(pick_last, lob, nt_dims, preferred_element_type=F32))
        done = jnp.where(tile_end[0:1, :] <= kcol, 1.0, 0.0).astype(BF16)
        full = lax.dot_general(ones_nt, done, nt_dims, preferred_element_type=F32)
        step_hi = (pltpu.roll(hi, nt - 1, 0) - hi).astype(BF16)
        step_lo = (pltpu.roll(lo, nt - 1, 0) - lo).astype(BF16)
        sel_hi = jnp.dot(done, step_hi, preferred_element_type=F32) + hi[0:1, :]
        sel_lo = jnp.dot(done, step_lo, preferred_element_type=F32) + lo[0:1, :]
        inside = jnp.where(32.0 * sel_hi + sel_lo <= kcol, 1.0, 0.0).astype(BF16)
        part = lax.dot_general(ones_ln, inside, nt_dims, preferred_element_type=F32)
        idx_ref[0, pl.ds(pl.multiple_of(ex * 8, 8), 8), :] = (LANES * full + part).astype(jnp.int32)
        return carry

    lax.fori_loop(0, e, expert_list, 0)


def _route(aff_t, cap):
    bsz, e, s = aff_t.shape
    nt = s // LANES
    assert nt & (nt - 1) == 0
    blk = pl.BlockSpec((1, e, nt, LANES), lambda b: (b, 0, 0, 0))
    idx8, pos = pl.pallas_call(
        functools.partial(_route_kernel, cap=cap),
        out_shape=(jax.ShapeDtypeStruct((bsz, e * 8, cap), jnp.int32),
                   jax.ShapeDtypeStruct((bsz, e, nt, LANES), jnp.int32)),
        grid=(bsz,),
        in_specs=[blk],
        out_specs=(pl.BlockSpec((1, e * 8, cap), lambda b: (b, 0, 0)), blk),
        scratch_shapes=[pltpu.VMEM((e, nt, LANES), F32)],
        compiler_params=_cparams(("parallel",)),
        name="route",
    )(aff_t.reshape(bsz, e, nt, LANES))
    return idx8[:, ::8, :], pos.reshape(bsz, e, s)


def _expert_kernel(idx_ref, h_hbm, wg_hbm, wu_hbm, wd_hbm, y_ref, xa, xb, sem, wg_s, wu_s, wd_s, stg_a, stg_d,
                   wsem, *, layer, cap, d):
    e = pl.program_id(0)
    b = pl.program_id(1)
    r = pl.program_id(2)
    n_e = pl.num_programs(0)
    n_b = pl.num_programs(1)
    rows = xa.shape[0]
    step = (e * n_b + b) * 2 + r
    n_steps = n_e * n_b * 2
    sub = b * 2 + r
    n_sub = n_b * 2
    ca = stg_a.shape[1]
    cd = stg_d.shape[1]

    def pieces(s):
        out = []
        for j in range(2):
            ra = pl.multiple_of((2 * s + j) * ca, ca)
            rd = pl.multiple_of((2 * s + j) * cd, cd)
            out.append((wg_hbm, wg_s, ra, ca, stg_a.at[j]))
            out.append((wu_hbm, wu_s, ra, ca, stg_a.at[2 + j]))
            out.append((wd_hbm, wd_s, rd, cd, stg_d.at[j]))
        return out

    def stage_copies(ee, s):
        return [pltpu.make_async_copy(src.at[layer, ee, pl.ds(r0, n), :], stg, wsem.at[0])
                for src, _, r0, n, stg in pieces(s)]

    def convert(dst_set, s):
        for _, dst, r0, n, stg in pieces(s):
            dst[dst_set, pl.ds(r0, n), :] = stg[...].astype(BF16)

    next_expert = lambda ee: jnp.where(ee == n_e - 1, 0, ee + 1)
    cur_set = e % 2

    def gather(bb, ee, rr, buf, slot, unrolled):
        base = (bb * n_e + ee) * cap + rr * rows

        def start(i):
            pltpu.make_async_copy(h_hbm.at[idx_ref[base + i]], buf.at[pl.ds(i, 1), :], sem.at[slot]).start()

        if unrolled:
            for i in range(rows):
                start(i)
        else:
            def body(i, carry):
                start(i)
                return carry
            lax.fori_loop(0, rows, body, 0)

    def wait(buf, slot):
        pltpu.make_async_copy(h_hbm.at[pl.ds(0, rows), 0, :], buf, sem.at[slot]).wait()

    wrap_b = r == 1
    wrap_e = wrap_b & (b == n_b - 1)
    b1 = jnp.where(wrap_b, jnp.where(b == n_b - 1, 0, b + 1), b)
    e1 = jnp.where(wrap_e, jnp.where(e == n_e - 1, 0, e + 1), e)

    @pl.when(step == 0)
    def _():
        gather(b, e, r, xa, 0, False)
        for s in range(n_sub):
            for cp in stage_copies(0, s):
                cp.start()
            for cp in stage_copies(0, s):
                cp.wait()
            convert(0, s)
        for cp in stage_copies(next_expert(e), sub):
            cp.start()

    def compute(cur, cur_slot, nxt, nxt_slot):
        gather(b1, e1, 1 - r, nxt, nxt_slot, True)
        wait(cur, cur_slot)
        x = cur[:, :d].astype(BF16)
        hg = jnp.dot(x, wg_s[cur_set], preferred_element_type=F32)
        for cp in stage_copies(next_expert(e), sub):
            cp.wait()
        convert(1 - cur_set, sub)
        hu = jnp.dot(x, wu_s[cur_set], preferred_element_type=F32)
        hid = (hg * jax.nn.sigmoid(hg) * hu).astype(BF16)
        lane = lax.broadcasted_iota(jnp.int32, (rows, LANES), 1)
        gate = jnp.sum(jnp.where(lane == e, cur[:, d:], 0.0), axis=-1, keepdims=True)
        half = d // 2
        for c in range(2):
            cols = slice(c * half, (c + 1) * half)
            y = jnp.dot(hid, wd_s[cur_set, :, cols], preferred_element_type=F32)
            y_ref[0, 0, :, cols] = (y * gate).astype(y_ref.dtype)

        @pl.when(step < n_steps - 1)
        def _():
            last_sub = sub == n_sub - 1
            e_next = jnp.where(last_sub, e + 1, e)
            for cp in stage_copies(next_expert(e_next), jnp.where(last_sub, 0, sub + 1)):
                cp.start()

    @pl.when(r == 0)
    def _():
        compute(xa, 0, xb, 1)

    @pl.when(r == 1)
    def _():
        compute(xb, 1, xa, 0)

    @pl.when(step == n_steps - 1)
    def _():
        wait(xa, 0)


def _experts(idx, h2, wg, wu, wd, layer):
    bsz, e, cap = idx.shape
    d = h2.shape[2] - LANES
    ff = wg.shape[3]
    rows = cap // 2
    n_pieces = 2 * 2 * bsz
    any_space = pl.BlockSpec(memory_space=pl.ANY)
    grid_spec = pltpu.PrefetchScalarGridSpec(
        num_scalar_prefetch=1,
        grid=(e, bsz, 2),
        in_specs=[any_space, any_space, any_space, any_space],
        out_specs=pl.BlockSpec((1, 1, rows, d), lambda ei, b, r, idx: (b, ei, r, 0)),
        scratch_shapes=[pltpu.VMEM((rows, d + LANES), F32), pltpu.VMEM((rows, d + LANES), F32),
                        pltpu.SemaphoreType.DMA((2,)),
                        pltpu.VMEM((2, d, ff), BF16), pltpu.VMEM((2, d, ff), BF16), pltpu.VMEM((2, ff, d), BF16),
                        pltpu.VMEM((4, d // n_pieces, ff), F32), pltpu.VMEM((2, ff // n_pieces, d), F32),
                        pltpu.SemaphoreType.DMA((1,))],
    )
    return pl.pallas_call(
        functools.partial(_expert_kernel, layer=layer, cap=cap, d=d),
        out_shape=jax.ShapeDtypeStruct((bsz, e, cap, d), BF16),
        grid_spec=grid_spec,
        compiler_params=pltpu.CompilerParams(dimension_semantics=("arbitrary", "arbitrary", "arbitrary"),
                                             vmem_limit_bytes=EXPERT_VMEM_LIMIT),
        name="experts",
    )(idx.reshape(-1), h2, wg, wu, wd)


def _combine_kernel(off_ref, x_ref, pos_ref, y_hbm, o_ref, ybuf, sem, *, cap, n_tiles):
    b = pl.program_id(0)
    i = pl.program_id(1)
    n_b = pl.num_programs(0)
    tt = x_ref.shape[0]
    n_exp = pos_ref.shape[1]
    ch = COMBINE_CHUNK
    per_tile = LANES // ch
    step = b * n_tiles + i
    slot = step % 2
    last_start = cap - ch

    def tile_rows(bb, ii):
        base = (bb * (n_tiles + 1) + ii) * n_exp
        first = [(off_ref[base + ex] // COMBINE_ALIGN) * COMBINE_ALIGN for ex in range(n_exp)]
        end = [off_ref[base + n_exp + ex] for ex in range(n_exp)]
        return first, end

    def chunk_start(lo):
        return pl.multiple_of(jnp.minimum(lo, last_start), COMBINE_ALIGN)

    def chunk_copy(bb, ex, lo, sl):
        return pltpu.make_async_copy(y_hbm.at[bb, ex, pl.ds(chunk_start(lo), ch), :],
                                     ybuf.at[sl, pl.ds(ex * ch, ch), :], sem.at[sl])

    def fetch(bb, los, sl):
        for ex in range(n_exp):
            chunk_copy(bb, ex, los[ex], sl).start()

    def wait(sl):
        for ex in range(n_exp):
            chunk_copy(0, ex, 0, sl).wait()

    lane = lax.broadcasted_iota(jnp.int32, (tt, LANES), 1)

    def onehot(los):
        blocks = []
        for j in range(n_exp // per_tile):
            target = jnp.full((tt, LANES), -1, jnp.int32)
            for u in range(per_tile):
                ex = j * per_tile + u
                p = pos_ref[:, ex:ex + 1]
                lo = los[ex]
                q = jnp.where((p >= lo) & (p < lo + ch), p - chunk_start(lo) + u * ch, -1)
                target = jnp.where((lane >= u * ch) & (lane < (u + 1) * ch), q, target)
            blocks.append(jnp.where(target == lane, 1.0, 0.0).astype(BF16))
        return jnp.concatenate(blocks, axis=1)

    first, end = tile_rows(b, i)

    @pl.when(step == 0)
    def _():
        fetch(b, first, slot)

    @pl.when(step < n_b * n_tiles - 1)
    def _():
        wrap = i == n_tiles - 1
        nxt, _ = tile_rows(jnp.where(wrap, b + 1, b), jnp.where(wrap, 0, i + 1))
        fetch(jnp.where(wrap, b + 1, b), nxt, 1 - slot)

    wait(slot)
    o_ref[...] = x_ref[...] + jnp.dot(onehot(first), ybuf[slot], preferred_element_type=F32)

    n_pass = functools.reduce(jnp.maximum, [(end[ex] - first[ex] + ch - 1) // ch for ex in range(n_exp)])

    def extra(c, carry):
        los = [first[ex] + c * ch for ex in range(n_exp)]
        fetch(b, los, slot)
        wait(slot)
        o_ref[...] += jnp.dot(onehot(los), ybuf[slot], preferred_element_type=F32)
        return carry

    lax.fori_loop(1, n_pass, extra, 0)


def _combine(x, pos, offsets, y, bsz, seq):
    d = x.shape[1]
    e, cap = y.shape[1], y.shape[2]
    tt = COMBINE_TOKENS
    n_tiles = seq // tt
    row = lambda b, i, off: (b * n_tiles + i, 0)
    grid_spec = pltpu.PrefetchScalarGridSpec(
        num_scalar_prefetch=1,
        grid=(bsz, n_tiles),
        in_specs=[
            pl.BlockSpec((tt, d), row),
            pl.BlockSpec((tt, e), row),
            pl.BlockSpec(memory_space=pl.ANY),
        ],
        out_specs=pl.BlockSpec((tt, d), row),
        scratch_shapes=[pltpu.VMEM((2, e * COMBINE_CHUNK, d), BF16), pltpu.SemaphoreType.DMA((2,))],
    )
    return pl.pallas_call(
        functools.partial(_combine_kernel, cap=cap, n_tiles=n_tiles),
        out_shape=jax.ShapeDtypeStruct(x.shape, F32),
        grid_spec=grid_spec,
        compiler_params=_cparams(("arbitrary", "arbitrary")),
        name="combine",
    )(offsets.reshape(-1), x, pos, y)


def _moe(x, h2, wg, wu, wd, layer, bsz, seq):
    d = x.shape[1]
    cap = CAPACITY_FACTOR * seq // N_EXPERTS
    aff_t = jnp.swapaxes(h2[:, 0, d:d + N_EXPERTS].reshape(bsz, seq, N_EXPERTS), 1, 2)
    idx, pos_t = _route(aff_t, cap)
    rows_of = idx + (jnp.arange(bsz, dtype=jnp.int32) * seq)[:, None, None]
    y = _experts(rows_of, h2, wg, wu, wd, layer)
    first = pos_t[:, :, ::COMBINE_TOKENS]
    offsets = jnp.concatenate(
        [jnp.where(first >= 0, first, -1 - first), jnp.full((bsz, N_EXPERTS, 1), cap, jnp.int32)], axis=-1)
    offsets = jnp.swapaxes(offsets, 1, 2)
    pos = jnp.swapaxes(pos_t, 1, 2).reshape(bsz * seq, N_EXPERTS)
    return _combine(x, pos, offsets, y, bsz, seq)


def kernel(x, mem, norm1_gain, w_in_attn, attn_q_gain, attn_k_gain, attn_sink, w_in_ret, ret_decay_fwd,
           ret_decay_bwd, ret_out_gain, mem_norm_gain, w_mem_kv, cross_q_gain, cross_k_gain, w_out,
           norm2_gain, w_router, w_gate, w_up, w_down):
    bsz, seq, d = x.shape
    mem_len = mem.shape[1]
    depth = norm1_gain.shape[0]
    xf = x.reshape(bsz * seq, d)
    memf = mem.reshape(bsz * mem_len, d)
    scale = HEAD_DIM ** -0.5
    ones = lambda n: jnp.ones((n,), F32)
    tile = lambda g, n: jnp.tile(g, n)
    w_in_attn, w_in_ret, w_mem_kv, w_out = (w.astype(BF16) for w in (w_in_attn, w_in_ret, w_mem_kv, w_out))
    for layer in range(depth):
        li = layer // 2
        kv_scale = jnp.concatenate([tile(cross_k_gain[layer], CROSS_HEADS), ones(CROSS_W)])
        kvm = _norm_matmul(memf, mem_norm_gain[layer], w_mem_kv, layer, kv_scale,
                           (0, CROSS_W, 0, 0), tm=bsz * mem_len)
        qc_scale = tile(cross_q_gain[layer] * scale, CROSS_HEADS)
        if layer % 2 == 0:
            post = jnp.concatenate([tile(attn_q_gain[li] * scale, ATTN_Q_HEADS),
                                    tile(attn_k_gain[li], ATTN_KV_HEADS), ones(ATTN_KV_W), qc_scale])
            q_end = ATTN_Q_W + ATTN_KV_W
            qc_start = q_end + ATTN_KV_W
            proj = _norm_matmul(xf, norm1_gain[layer], w_in_attn, li, post,
                                (0, q_end, qc_start, qc_start + CROSS_W), tm=1024)
            mix_inputs = (_win_attn(proj, attn_sink[li], bsz, seq),)
        else:
            post = jnp.concatenate([ones(RET_QK_W), jnp.full((RET_QK_W,), RET_DK ** -0.5, F32),
                                    ones(2 * RET_V_W), qc_scale])
            qc_start = 2 * RET_QK_W + 2 * RET_V_W
            proj = _norm_matmul(xf, norm1_gain[layer], w_in_ret, li, post,
                                (qc_start, qc_start + CROSS_W, 0, 0), tm=1024)
            fwd, bwd = _retention(proj, ret_decay_fwd[li], ret_decay_bwd[li], bsz, seq)
            gate_col = (2 * RET_QK_W + RET_V_W) // RET_V_W
            mix_inputs = (fwd, bwd, proj, gate_col, ret_out_gain[li])
        cross = _cross_attn(proj, qc_start // CROSS_W, kvm, bsz, seq, mem_len)
        xf, h2 = _out_proj(xf, mix_inputs, cross, w_out, layer, norm2_gain[layer], w_router[layer])
        xf = _moe(xf, h2, w_gate, w_up, w_down, layer, bsz, seq)
    return xf.reshape(bsz, seq, d)
```
